```python
import jax
import jax.numpy as jnp
from jax import lax
import numpy as np

D_MODEL = 2048
BATCH = 16
SEQ = 2048
DEPTH = 2

CONV_WIDTH = 1024
CONV_KERNEL = 31
RWKV_HEAD_SIZE = 64
RWKV_HEADS = 32
RWKV_WIDTH = RWKV_HEADS * RWKV_HEAD_SIZE
R_DECAY = 96
R_AAA = 96
R_GATE = 256
R_MV = 64
RWKV_COLS = 3 * RWKV_WIDTH + R_DECAY + R_AAA + R_GATE
IN_COLS = 2 * CONV_WIDTH + RWKV_COLS + 2 * D_MODEL
N_EXPERTS = 32
TOP_K = 4
D_EXPERT = 1024
SWIGLU_LIMIT = 7.0
SWIGLU_ALPHA = 1.702
EXPERT_BLOCK = 128

RMS_EPS = 1e-5
LN_EPS = 1e-5
GN_EPS = 64e-5

kernel_name = 'cond_hybrid_conv_rwkv7_moe'


def rmsnorm(x, g):
    xf = x.astype(jnp.float32)
    xf = xf * lax.rsqrt(jnp.mean(xf * xf, axis=-1, keepdims=True) + RMS_EPS)
    return xf.astype(x.dtype) * g


def layernorm(x, g, b):
    xf = x.astype(jnp.float32)
    mu = jnp.mean(xf, axis=-1, keepdims=True)
    var = jnp.mean(jnp.square(xf - mu), axis=-1, keepdims=True)
    return ((xf - mu) * lax.rsqrt(var + LN_EPS)).astype(x.dtype) * g + b


def token_shift(z, mu):
    prev = jnp.pad(z[:, :-1], ((0, 0), (1, 0), (0, 0)))
    return z + (prev - z) * mu


def conformer_conv_branch(glu_in, conv_w, conv_b, ln_g, ln_b, w_proj):
    a = glu_in[..., :CONV_WIDTH] * jax.nn.sigmoid(glu_in[..., CONV_WIDTH:])
    a = lax.conv_general_dilated(
        a, conv_w.astype(a.dtype), window_strides=(1,),
        padding=[(CONV_KERNEL - 1, 0)],
        dimension_numbers=('NWC', 'WIO', 'NWC'),
        feature_group_count=CONV_WIDTH) + conv_b
    a = jax.nn.silu(layernorm(a, ln_g, ln_b))
    return a @ w_proj


def wkv7_scan(r, decay, k, v, a, b):
    bsz, _, h, n = r.shape
    xs = tuple(jnp.moveaxis(t, 1, 0) for t in (r, decay, k, v, a, b))

    def step(state, inp):
        r_t, d_t, k_t, v_t, a_t, b_t = inp
        sa = jnp.einsum('bhij,bhj->bhi', state, a_t)
        state = (state * d_t[:, :, None, :]
                 + v_t[..., :, None] * k_t[..., None, :]
                 + sa[..., :, None] * b_t[..., None, :])
        return state, jnp.einsum('bhij,bhj->bhi', state, r_t)

    state0 = jnp.zeros((bsz, h, n, n), jnp.float32)
    _, ys = lax.scan(step, state0, xs)
    return jnp.moveaxis(ys, 0, 1)


def rwkv7_branch(z, u, w0, w2, a0, a2, g2, k_k, k_a, r_k, gn_g, gn_b, w_proj, v_first, vres):
    bsz, seq, _ = z.shape
    cw = RWKV_WIDTH
    r = z[..., :cw]
    k = z[..., cw:2 * cw]
    v = z[..., 2 * cw:3 * cw]
    o = 3 * cw
    zw = z[..., o:o + R_DECAY]
    o += R_DECAY
    za = z[..., o:o + R_AAA]
    o += R_AAA
    zg = z[..., o:o + R_GATE]
    w = -jax.nn.softplus(-(w0 + jnp.tanh(zw) @ w2)) - 0.5
    a = jax.nn.sigmoid(a0 + za @ a2)
    g = jax.nn.sigmoid(zg) @ g2
    if vres is None:
        v_first = v
    else:
        v0, v1, mu_v, v2 = vres
        zv = token_shift(u @ v1, mu_v)
        v = v + (v_first - v) * jax.nn.sigmoid(v0 + zv @ v2)
    kk = k * k_k
    k = k * (1 + (a - 1) * k_a)

    def heads(t):
        return t.astype(jnp.float32).reshape(bsz, seq, RWKV_HEADS, RWKV_HEAD_SIZE)

    rh, kh, vh, ah, wh, kkh = heads(r), heads(k), heads(v), heads(a), heads(w), heads(kk)
    kkh = kkh / jnp.maximum(jnp.linalg.norm(kkh, axis=-1, keepdims=True), 1e-12)
    y = wkv7_scan(rh, jnp.exp(-jnp.exp(wh)), kh, vh, -kkh, kkh * ah)
    mu = jnp.mean(y, axis=-1, keepdims=True)
    var = jnp.mean(jnp.square(y - mu), axis=-1, keepdims=True)
    hshape = (RWKV_HEADS, RWKV_HEAD_SIZE)
    y = ((y - mu) * lax.rsqrt(var + GN_EPS) * gn_g.reshape(hshape).astype(jnp.float32)
         + gn_b.reshape(hshape).astype(jnp.float32))
    y = y + jnp.sum(rh * kh * r_k.reshape(hshape).astype(jnp.float32), axis=-1, keepdims=True) * vh
    y = y.reshape(bsz, seq, cw).astype(u.dtype)
    return (y * g) @ w_proj, v_first


def moe_ffn(h, router_w, router_b, w_gate_up, b_gate_up, w_down, b_down):
    bsz, seq, d = h.shape
    t = bsz * seq
    hf = h.reshape(t, d)
    logits = (hf @ router_w + router_b).astype(jnp.float32)
    top_logits, top_idx = lax.top_k(logits, TOP_K)
    top_w = jax.nn.softmax(top_logits, axis=-1).astype(h.dtype)
    m = t * TOP_K
    flat_e = top_idx.reshape(m)
    flat_tok = jnp.arange(m, dtype=jnp.int32) // TOP_K
    order = jnp.argsort(flat_e)
    sorted_e = flat_e[order]
    counts = jnp.bincount(flat_e, length=N_EXPERTS)
    padded = (counts + EXPERT_BLOCK - 1) // EXPERT_BLOCK * EXPERT_BLOCK
    starts = jnp.cumsum(counts) - counts
    pad_ends = jnp.cumsum(padded)
    pad_starts = pad_ends - padded
    dest = pad_starts[sorted_e] + (jnp.arange(m, dtype=jnp.int32) - starts[sorted_e])
    n_blocks = -(-m // EXPERT_BLOCK) + N_EXPERTS
    p = n_blocks * EXPERT_BLOCK
    tok_buf = jnp.full((p,), t, jnp.int32).at[dest].set(flat_tok[order])
    w_buf = jnp.zeros((p,), h.dtype).at[dest].set(top_w.reshape(m)[order])
    block_start = jnp.arange(n_blocks, dtype=jnp.int32) * EXPERT_BLOCK
    block_expert = jnp.minimum(jnp.searchsorted(pad_ends, block_start, side='right'), N_EXPERTS - 1)
    h_pad = jnp.concatenate([hf, jnp.zeros((1, d), h.dtype)], axis=0)

    def expert_block(args):
        idx, e = args
        gu = h_pad[idx] @ w_gate_up[e] + b_gate_up[e]
        gate = jnp.minimum(gu[:, :D_EXPERT], SWIGLU_LIMIT)
        up = jnp.clip(gu[:, D_EXPERT:], -SWIGLU_LIMIT, SWIGLU_LIMIT)
        return ((up + 1) * gate * jax.nn.sigmoid(SWIGLU_ALPHA * gate)) @ w_down[e] + b_down[e]

    y_buf = lax.map(expert_block, (tok_buf.reshape(n_blocks, EXPERT_BLOCK), block_expert))
    y = jax.ops.segment_sum(y_buf.reshape(p, d) * w_buf[:, None], tok_buf, num_segments=t + 1)
    return y[:t].reshape(bsz, seq, d)


def setup_inputs(seed: int = 0) -> dict:
    key = jax.random.key(seed)
    keys = iter(jax.random.split(key, 40))
    f32 = jnp.float32

    def nrm(shape, scale):
        return jax.random.normal(next(keys), shape, f32) * scale

    def unif(shape, lo, hi):
        return jax.random.uniform(next(keys), shape, f32, lo, hi)

    L, D, CA, CB = DEPTH, D_MODEL, CONV_WIDTH, RWKV_WIDTH
    LV = DEPTH - 1
    return {
        'x': nrm((BATCH, SEQ, D), 1.0),
        'c': nrm((BATCH, D), 1.0),
        'norm1_g': 1.0 + nrm((L, D), 0.02),
        'norm2_g': 1.0 + nrm((L, D), 0.02),
        'w_mod': nrm((L, D, 6 * D), 0.5 * D ** -0.5),
        'b_mod': nrm((L, 6 * D), 0.02),
        'w_in': nrm((L, D, IN_COLS), D ** -0.5),
        'conv_w': nrm((L, CONV_KERNEL, 1, CA), CONV_KERNEL ** -0.5),
        'conv_b': nrm((L, CA), 0.02),
        'conv_ln_g': 1.0 + nrm((L, CA), 0.02),
        'conv_ln_b': nrm((L, CA), 0.02),
        'w_conv_proj': nrm((L, CA, D), CA ** -0.5),
        'mu_shift': unif((L, RWKV_COLS), 0.0, 1.0),
        'w0': unif((L, CB), -5.0, -0.5),
        'w2': nrm((L, R_DECAY, CB), 0.5 * R_DECAY ** -0.5),
        'a0': nrm((L, CB), 0.1),
        'a2': nrm((L, R_AAA, CB), R_AAA ** -0.5),
        'g2': nrm((L, R_GATE, CB), R_GATE ** -0.5),
        'k_k': 0.85 + nrm((L, CB), 0.05),
        'k_a': 1.0 + nrm((L, CB), 0.05),
        'r_k': nrm((L, CB), 0.1),
        'gn_g': 1.0 + nrm((L, CB), 0.02),
        'gn_b': nrm((L, CB), 0.02),
        'w_rwkv_proj': nrm((L, CB, D), CB ** -0.5),
        'v0': nrm((LV, CB), 0.1),
        'v1': nrm((LV, D, R_MV), D ** -0.5),
        'mu_v': unif((LV, R_MV), 0.0, 1.0),
        'v2': nrm((LV, R_MV, CB), R_MV ** -0.5),
        'w_out': nrm((L, D, D), D ** -0.5),
        'router_w': nrm((L, D, N_EXPERTS), D ** -0.5),
        'router_b': nrm((L, N_EXPERTS), 0.01),
        'w_gate_up': nrm((L, N_EXPERTS, D, 2 * D_EXPERT), D ** -0.5),
        'b_gate_up': nrm((L, N_EXPERTS, 2 * D_EXPERT), 0.02),
        'w_down': nrm((L, N_EXPERTS, D_EXPERT, D), D_EXPERT ** -0.5),
        'b_down': nrm((L, N_EXPERTS, D), 0.02),
        'final_g': 1.0 + nrm((D,), 0.02),
    }


def reference(x, c, norm1_g, norm2_g, w_mod, b_mod, w_in, conv_w, conv_b, conv_ln_g, conv_ln_b,
              w_conv_proj, mu_shift, w0, w2, a0, a2, g2, k_k, k_a, r_k, gn_g, gn_b, w_rwkv_proj,
              v0, v1, mu_v, v2, w_out, router_w, router_b, w_gate_up, b_gate_up, w_down, b_down,
              final_g):
    cond = jax.nn.silu(c)
    v_first = None
    c0 = 2 * CONV_WIDTH
    c1 = c0 + RWKV_COLS
    for l in range(DEPTH):
        mod = (cond @ w_mod[l] + b_mod[l])[:, None, :]
        sh1, sc1, gt1, sh2, sc2, gt2 = jnp.split(mod, 6, axis=-1)
        u = rmsnorm(x, norm1_g[l]) * (1 + sc1) + sh1
        proj = u @ w_in[l]
        y_a = conformer_conv_branch(proj[..., :c0], conv_w[l], conv_b[l], conv_ln_g[l],
                                    conv_ln_b[l], w_conv_proj[l])
        z = token_shift(proj[..., c0:c1], mu_shift[l])
        vres = None if l == 0 else (v0[l - 1], v1[l - 1], mu_v[l - 1], v2[l - 1])
        y_b, v_first = rwkv7_branch(z, u, w0[l], w2[l], a0[l], a2[l], g2[l], k_k[l], k_a[l],
                                    r_k[l], gn_g[l], gn_b[l], w_rwkv_proj[l], v_first, vres)
        gates = jax.nn.sigmoid(proj[..., c1:])
        mixed = (gates[..., :D_MODEL] * y_a + gates[..., D_MODEL:] * y_b) @ w_out[l]
        x = x + gt1 * mixed
        h = rmsnorm(x, norm2_g[l]) * (1 + sc2) + sh2
        x = x + gt2 * moe_ffn(h, router_w[l], router_b[l], w_gate_up[l], b_gate_up[l],
                              w_down[l], b_down[l])
    return rmsnorm(x, final_g)
```

```python
import functools

import jax
import jax.numpy as jnp
from jax import lax
from jax.experimental import pallas as pl
from jax.experimental.pallas import tpu as pltpu

F32 = jnp.float32
BF = jnp.bfloat16

HEAD = 64
CHUNK = 64
TOP_K = 4
RMS_EPS = 1e-5
LN_EPS = 1e-5
GN_EPS = 64e-5
SWIGLU_LIMIT = 7.0
SWIGLU_ALPHA = 1.702
LORA_W = 256
LANE = 128
CONV_HALO = 32
SHIFT_HALO = 16
VMEM_LIMIT = 56 * 1024 * 1024


def _params(*sem):
    return pltpu.CompilerParams(dimension_semantics=sem, vmem_limit_bytes=VMEM_LIMIT)


def _tile(n, pref):
    t = min(n, pref)
    while n % t:
        t -= 8
    return t


def _split2(x):
    hi = x.astype(BF)
    lo = (x - hi.astype(F32)).astype(BF)
    return hi, lo


def _split3(x):
    hi = x.astype(BF)
    r1 = x - hi.astype(F32)
    mid = r1.astype(BF)
    lo = (r1 - mid.astype(F32)).astype(BF)
    return hi, mid, lo


def _dot(a, b):
    return jnp.dot(a, b, preferred_element_type=F32)


def _dot_nt(a, b):
    return lax.dot_general(a, b, (((1,), (1,)), ((), ())), preferred_element_type=F32)


def _dot_tn(a, b):
    return lax.dot_general(a, b, (((0,), (0,)), ((), ())), preferred_element_type=F32)


def _sigmoid(x):
    return 1.0 / (1.0 + jnp.exp(-x))


def _mod_kernel(c_ref, w_ref, b_ref, o_ref):
    c = c_ref[...]
    cond = (c * _sigmoid(c)).astype(BF)
    o_ref[...] = _dot(cond, w_ref[...].astype(BF)) + b_ref[...]


def _modulation(c, w_mod, b_mod):
    nl, d, n6 = w_mod.shape
    b = c.shape[0]
    tn = _tile(n6, 1024)
    return pl.pallas_call(
        _mod_kernel,
        grid=(nl, n6 // tn),
        in_specs=[pl.BlockSpec((b, d), lambda l, j: (0, 0)),
                  pl.BlockSpec((None, d, tn), lambda l, j: (l, 0, j)),
                  pl.BlockSpec((None, 1, tn), lambda l, j: (l, 0, j))],
        out_specs=pl.BlockSpec((None, b, tn), lambda l, j: (l, 0, j)),
        out_shape=jax.ShapeDtypeStruct((nl, b, n6), F32),
        compiler_params=_params("parallel", "parallel"),
        name="modulation",
    )(c, w_mod, b_mod.reshape(nl, 1, n6))


def _inproj_kernel(x_ref, sh_ref, sc_ref, g_ref, w_ref, o_ref, u_ref):
    @pl.when(pl.program_id(1) == 0)
    def _():
        x = x_ref[...]
        ms = jnp.mean(x * x, axis=-1, keepdims=True)
        xn = x * lax.rsqrt(ms + RMS_EPS) * g_ref[...]
        u_ref[...] = (xn * (1.0 + sc_ref[...]) + sh_ref[...]).astype(BF)

    o_ref[...] = _dot(u_ref[...], w_ref[...]).astype(o_ref.dtype)


def _inproj(x2, mod3, g, w, seq):
    t, d = x2.shape
    n = w.shape[1]
    tm = _tile(seq, 1024)
    tn = _tile(n, 512)
    bidx = lambda i: (i * tm) // seq
    return pl.pallas_call(
        _inproj_kernel,
        grid=(t // tm, n // tn),
        in_specs=[pl.BlockSpec((tm, d), lambda i, j: (i, 0)),
                  pl.BlockSpec((None, 1, d), lambda i, j: (bidx(i), 0, 0)),
                  pl.BlockSpec((None, 1, d), lambda i, j: (bidx(i), 0, 1)),
                  pl.BlockSpec((1, d), lambda i, j: (0, 0)),
                  pl.BlockSpec((d, tn), lambda i, j: (0, j))],
        out_specs=pl.BlockSpec((tm, tn), lambda i, j: (i, j)),
        out_shape=jax.ShapeDtypeStruct((t, n), BF),
        scratch_shapes=[pltpu.VMEM((tm, d), BF)],
        compiler_params=_params("parallel", "arbitrary"),
        name="inproj",
    )(x2, mod3, mod3, g.reshape(1, d), w)


def _conv_kernel(p_ref, h_ref, w_ref, cb_ref, g_ref, b_ref, o_ref, a_ref, y_ref, *, ts, ca, taps, seq):
    i = pl.program_id(0)
    p = p_ref[...].astype(F32)
    a_ref[pl.ds(CONV_HALO, ts), :] = p[:, :ca] * _sigmoid(p[:, ca:])
    hp = h_ref[...].astype(F32)
    keep = jnp.where((i * ts) % seq == 0, 0.0, 1.0)
    a_ref[pl.ds(0, CONV_HALO), :] = hp[:, :ca] * _sigmoid(hp[:, ca:]) * keep

    rb = min(ts, 64)
    off = CONV_HALO - (taps - 1)

    for r0 in range(0, ts, rb):
        for c0 in range(0, ca, LANE):
            acc = jnp.zeros((rb, LANE), F32)
            for j in range(taps):
                acc = acc + a_ref[r0 + off + j:r0 + off + j + rb, c0:c0 + LANE] * w_ref[j:j + 1, c0:c0 + LANE]
            y_ref[r0:r0 + rb, c0:c0 + LANE] = acc
    y = y_ref[...] + cb_ref[...]
    mu = jnp.mean(y, axis=-1, keepdims=True)
    yc = y - mu
    var = jnp.mean(yc * yc, axis=-1, keepdims=True)
    yn = yc * lax.rsqrt(var + LN_EPS) * g_ref[...] + b_ref[...]
    o_ref[...] = (yn * _sigmoid(yn)).astype(o_ref.dtype)


def _conv_branch(proj, conv_w, conv_b, ln_g, ln_b, seq):
    t = proj.shape[0]
    taps, _, ca = conv_w.shape
    assert taps - 1 <= CONV_HALO
    ts = _tile(seq, 256)
    hb = ts // CONV_HALO
    kern = functools.partial(_conv_kernel, ts=ts, ca=ca, taps=taps, seq=seq)
    return pl.pallas_call(
        kern,
        grid=(t // ts,),
        in_specs=[pl.BlockSpec((ts, 2 * ca), lambda i: (i, 0)),
                  pl.BlockSpec((CONV_HALO, 2 * ca), lambda i: (jnp.maximum(i * hb - 1, 0), 0)),
                  pl.BlockSpec((taps, ca), lambda i: (0, 0)),
                  pl.BlockSpec((1, ca), lambda i: (0, 0)),
                  pl.BlockSpec((1, ca), lambda i: (0, 0)),
                  pl.BlockSpec((1, ca), lambda i: (0, 0))],
        out_specs=pl.BlockSpec((ts, ca), lambda i: (i, 0)),
        out_shape=jax.ShapeDtypeStruct((t, ca), BF),
        scratch_shapes=[pltpu.VMEM((ts + CONV_HALO, ca), F32), pltpu.VMEM((ts, ca), F32)],
        compiler_params=_params("parallel"),
        name="conv_branch",
    )(proj, proj, conv_w.reshape(taps, ca), conv_b.reshape(1, ca), ln_g.reshape(1, ca), ln_b.reshape(1, ca))


def _token_shift(cur_ref, halo_ref, mu, first):
    cur = cur_ref[...].astype(F32)
    last = halo_ref[SHIFT_HALO - 1:SHIFT_HALO, :].astype(F32)
    last = jnp.where(first, 0.0, last)
    row = lax.broadcasted_iota(jnp.int32, cur.shape, 0)
    prev = jnp.where(row == 0, last, pltpu.roll(cur, 1, axis=0))
    return cur + (prev - cur) * mu


def _prep_kernel(*refs, ts, seq, cb, r_decay, has_vres):
    (r_ref, k_ref, v_ref, l_ref, rh_ref, kh_ref, vh_ref, lh_ref, mur_ref, muk_ref, muv_ref, mul_ref,
     w2_ref, a2_ref, g2_ref, w0_ref, a0_ref, kk_ref, ka_ref, e_ref, et_ref) = refs[:21]
    rest = refs[21:]
    if has_vres:
        v2_ref, v0_ref, vf_ref = rest[:3]
        rest = rest[3:]
    ro_ref, ldo_ref, ko_ref, vo_ref, kko_ref, ao_ref, go_ref = rest

    first = (pl.program_id(0) * ts) % seq == 0
    r = _token_shift(r_ref, rh_ref, mur_ref[...], first)
    k = _token_shift(k_ref, kh_ref, muk_ref[...], first)
    v = _token_shift(v_ref, vh_ref, muv_ref[...], first)
    lz = _token_shift(l_ref, lh_ref, mul_ref[...], first)

    l1 = lz[:, :LORA_W]
    lane = lax.broadcasted_iota(jnp.int32, l1.shape, 1)
    t1 = jnp.where(lane < r_decay, jnp.tanh(l1), l1).astype(BF)
    wpre = w0_ref[...] + _dot(t1, w2_ref[...])
    w = -(jnp.maximum(-wpre, 0.0) + jnp.log(1.0 + jnp.exp(-jnp.abs(wpre)))) - 0.5
    ldo_ref[...] = -jnp.exp(w)
    a = _sigmoid(a0_ref[...] + _dot(t1, a2_ref[...]))
    go_ref[...] = _dot(_sigmoid(lz[:, LORA_W:]).astype(BF), g2_ref[...]).astype(go_ref.dtype)
    if has_vres:
        v = v + (vf_ref[...] - v) * _sigmoid(v0_ref[...] + _dot(t1, v2_ref[...]))

    kk = k * kk_ref[...]
    sq_hi, sq_lo = _split2(kk * kk)
    ss = _dot(sq_hi, e_ref[...]) + _dot(sq_lo, e_ref[...])
    inv = lax.rsqrt(jnp.maximum(ss, 1e-24))
    inv_hi, inv_mid, inv_lo = _split3(inv)
    invb = _dot(inv_hi, et_ref[...]) + _dot(inv_mid, et_ref[...]) + _dot(inv_lo, et_ref[...])
    ro_ref[...] = r
    ko_ref[...] = k * (1.0 + (a - 1.0) * ka_ref[...])
    vo_ref[...] = v
    kko_ref[...] = kk * invb
    ao_ref[...] = a


def _rwkv_prep(proj, mu_rkv, mu_lora, w2p, a2p, g2, w0, a0, k_k, k_a, vres, seq, cb, col_r, col_lora, r_decay):
    t = proj.shape[0]
    ts = _tile(seq, 256)
    hb = ts // SHIFT_HALO
    nh = cb // HEAD
    hp = max(nh, LANE)
    head_of = jnp.arange(cb, dtype=jnp.int32) // HEAD
    e = (head_of[:, None] == jnp.arange(hp, dtype=jnp.int32)[None, :]).astype(BF)
    et = e.T
    lw = 2 * LORA_W
    cr = col_r // cb
    cl = col_lora // lw
    halo = lambda i: jnp.maximum(i * hb - 1, 0)
    row = lambda n: pl.BlockSpec((1, n), lambda i: (0, 0))
    full = lambda a, b: pl.BlockSpec((a, b), lambda i: (0, 0))
    in_specs = [pl.BlockSpec((ts, cb), lambda i: (i, cr)),
                pl.BlockSpec((ts, cb), lambda i: (i, cr + 1)),
                pl.BlockSpec((ts, cb), lambda i: (i, cr + 2)),
                pl.BlockSpec((ts, lw), lambda i: (i, cl)),
                pl.BlockSpec((SHIFT_HALO, cb), lambda i: (halo(i), cr)),
                pl.BlockSpec((SHIFT_HALO, cb), lambda i: (halo(i), cr + 1)),
                pl.BlockSpec((SHIFT_HALO, cb), lambda i: (halo(i), cr + 2)),
                pl.BlockSpec((SHIFT_HALO, lw), lambda i: (halo(i), cl)),
                row(cb), row(cb), row(cb), row(lw),
                full(LORA_W, cb), full(LORA_W, cb), full(LORA_W, cb),
                row(cb), row(cb), row(cb), row(cb), full(cb, hp), full(hp, cb)]
    args = [proj, proj, proj, proj, proj, proj, proj, proj,
            mu_rkv[0:cb].reshape(1, cb), mu_rkv[cb:2 * cb].reshape(1, cb), mu_rkv[2 * cb:3 * cb].reshape(1, cb),
            mu_lora.reshape(1, lw), w2p, a2p, g2, w0.reshape(1, cb), a0.reshape(1, cb),
            k_k.reshape(1, cb), k_a.reshape(1, cb), e, et]
    if vres is not None:
        v2p, v0, v_first = vres
        in_specs += [full(LORA_W, cb), row(cb), pl.BlockSpec((ts, cb), lambda i: (i, 0))]
        args += [v2p, v0.reshape(1, cb), v_first]
    tile = pl.BlockSpec((ts, cb), lambda i: (i, 0))
    kern = functools.partial(_prep_kernel, ts=ts, seq=seq, cb=cb, r_decay=r_decay, has_vres=vres is not None)
    f32o = jax.ShapeDtypeStruct((t, cb), F32)
    return pl.pallas_call(
        kern,
        grid=(t // ts,),
        in_specs=in_specs,
        out_specs=[tile] * 7,
        out_shape=[f32o] * 6 + [jax.ShapeDtypeStruct((t, cb), BF)],
        compiler_params=_params("parallel"),
        name="rwkv_prep",
    )(*args)


def _scan_kernel(r_ref, ld_ref, k_ref, v_ref, kk_ref, a_ref, g_ref, rk_ref, gg_ref, gb_ref, o_ref, s_ref,
                 *, nchunk, lg):
    c = CHUNK
    ng = lg // HEAD

    @pl.when(pl.program_id(2) == 0)
    def _():
        s_ref[...] = jnp.zeros_like(s_ref)

    row = lax.broadcasted_iota(jnp.int32, (lg, lg), 0)
    col = lax.broadcasted_iota(jnp.int32, (lg, lg), 1)
    same = (row // c) == (col // HEAD)
    bd = same.astype(F32)
    strict = same & (row > col)
    incl = same & (row >= col)
    eye = (row == col).astype(F32)
    avg = (bd * (1.0 / HEAD)).astype(BF)
    ones = bd.astype(BF)
    tr = lax.broadcasted_iota(jnp.int32, (c, c), 0)
    tc = lax.broadcasted_iota(jnp.int32, (c, c), 1)
    ltri = (tr >= tc).astype(BF)

    def stack(x):
        return jnp.concatenate([x] * ng, axis=0) * bd

    def body(ci, carry):
        sl = pl.ds(pl.multiple_of(ci * c, c), c)
        r = r_ref[sl, :]
        ld = ld_ref[sl, :]
        k = k_ref[sl, :]
        v = v_ref[sl, :]
        kk = kk_ref[sl, :]
        a = a_ref[sl, :]
        l_hi, l_mid, l_lo = _split3(ld)
        cum = _dot(ltri, l_hi) + _dot(ltri, l_mid) + _dot(ltri, l_lo)
        dinc = jnp.exp(cum)
        dinv = jnp.exp(-cum)
        dexc = jnp.exp(cum - ld)
        dtot = dinc[c - 1:c, :]
        a_s = stack(-kk * dexc)
        r_s = stack(r * dinc)
        b_s = stack(kk * a * dinv)
        k_s = stack(k * dinv)
        v_s = stack(v).astype(BF)
        a_b, r_b, b_b, k_b = a_s.astype(BF), r_s.astype(BF), b_s.astype(BF), k_s.astype(BF)

        a_ab = jnp.where(strict, _dot_nt(a_b, b_b), 0.0)
        a_ak = jnp.where(strict, _dot_nt(a_b, k_b), 0.0).astype(BF)
        a_rb = jnp.where(incl, _dot_nt(r_b, b_b), 0.0).astype(BF)
        a_rk = jnp.where(incl, _dot_nt(r_b, k_b), 0.0).astype(BF)

        tm = eye + a_ab
        pw = a_ab
        steps = c.bit_length() - 2
        for _ in range(steps):
            pb = pw.astype(BF)
            pw = _dot(pb, pb)
            tm = tm + _dot(tm.astype(BF), pw.astype(BF))
        tmb = tm.astype(BF)

        w = _dot(tmb, a_b)
        u0 = _dot(tmb, _dot(a_ak, v_s).astype(BF))
        wb = w.astype(BF)
        u0b = u0.astype(BF)
        q = r_s + _dot(a_rb, wb)
        y0 = _dot(a_rb, u0b) + _dot(a_rk, v_s)
        bd_t = (b_s * dtot).T.astype(BF)
        kd_t = (k_s * dtot).T.astype(BF)
        m = eye * dtot + _dot(bd_t, wb)
        pm = _dot(bd_t, u0b) + _dot(kd_t, v_s)

        s0 = s_ref[...].astype(BF)
        y = _dot(q.astype(BF), s0) + y0
        s_ref[...] = _dot(m.astype(BF), s0) + pm

        yt = y[0:c]
        for gi in range(1, ng):
            yt = yt + y[gi * c:(gi + 1) * c]

        y_hi, y_lo = _split2(yt)
        mu = _dot(y_hi, avg) + _dot(y_lo, avg)
        yc = yt - mu
        s_hi, s_lo = _split2(yc * yc)
        var = _dot(s_hi, avg) + _dot(s_lo, avg)
        yn = yc * lax.rsqrt(var + GN_EPS) * gg_ref[...] + gb_ref[...]
        b_hi, b_lo = _split2(r * k * rk_ref[...])
        bonus = _dot(b_hi, ones) + _dot(b_lo, ones)
        o_ref[sl, :] = ((yn + bonus * v) * g_ref[sl, :].astype(F32)).astype(o_ref.dtype)
        return carry

    lax.fori_loop(0, nchunk, body, 0)


def _wkv_scan(r, ld, k, v, kk, a, g, r_k, gn_g, gn_b, bsz, seq):
    t, cb = r.shape
    lg = min(cb, 4 * HEAD)
    sb = _tile(seq, 512)
    nsb = seq // sb
    tile = pl.BlockSpec((sb, lg), lambda b, h, s: (b * nsb + s, h))
    vec = pl.BlockSpec((1, lg), lambda b, h, s: (0, h))
    kern = functools.partial(_scan_kernel, nchunk=sb // CHUNK, lg=lg)
    return pl.pallas_call(
        kern,
        grid=(bsz, cb // lg, nsb),
        in_specs=[tile] * 7 + [vec] * 3,
        out_specs=tile,
        out_shape=jax.ShapeDtypeStruct((t, cb), BF),
        scratch_shapes=[pltpu.VMEM((lg, lg), F32)],
        compiler_params=_params("parallel", "parallel", "arbitrary"),
        name="wkv_scan",
    )(r, ld, k, v, kk, a, g, r_k.reshape(1, cb), gn_g.reshape(1, cb), gn_b.reshape(1, cb))


def _mixa_kernel(a_ref, y_ref, ga_ref, gb_ref, wc_ref, wr_ref, o_ref):
    ya = _dot(a_ref[...], wc_ref[...])
    yb = _dot(y_ref[...], wr_ref[...])
    m = _sigmoid(ga_ref[...].astype(F32)) * ya + _sigmoid(gb_ref[...].astype(F32)) * yb
    o_ref[...] = m.astype(o_ref.dtype)


def _mix_branches(a_conv, yg, proj, wc, wr, col_gate):
    t, ca = a_conv.shape
    cb = yg.shape[1]
    d = wc.shape[1]
    tm = _tile(t, 512)
    cg = col_gate // d
    return pl.pallas_call(
        _mixa_kernel,
        grid=(t // tm,),
        in_specs=[pl.BlockSpec((tm, ca), lambda i: (i, 0)),
                  pl.BlockSpec((tm, cb), lambda i: (i, 0)),
                  pl.BlockSpec((tm, d), lambda i: (i, cg)),
                  pl.BlockSpec((tm, d), lambda i: (i, cg + 1)),
                  pl.BlockSpec((ca, d), lambda i: (0, 0)),
                  pl.BlockSpec((cb, d), lambda i: (0, 0))],
        out_specs=pl.BlockSpec((tm, d), lambda i: (i, 0)),
        out_shape=jax.ShapeDtypeStruct((t, d), BF),
        compiler_params=_params("parallel"),
        name="mix_branches",
    )(a_conv, yg, proj, proj, wc, wr)


def _mixb_kernel(m_ref, x_ref, gt_ref, sh_ref, sc_ref, g_ref, wo_ref, rwh_ref, rwl_ref, rb_ref,
                 xo_ref, ho_ref, ti_ref, tw_ref, *, ne):
    x = x_ref[...] + gt_ref[...] * _dot(m_ref[...], wo_ref[...])
    xo_ref[...] = x
    ms = jnp.mean(x * x, axis=-1, keepdims=True)
    h = x * lax.rsqrt(ms + RMS_EPS) * g_ref[...] * (1.0 + sc_ref[...]) + sh_ref[...]
    ho_ref[...] = h
    h_hi, h_lo = _split2(h)
    logits = _dot(h_hi, rwh_ref[...]) + _dot(h_lo, rwh_ref[...]) + _dot(h_hi, rwl_ref[...]) + rb_ref[...]
    lane = lax.broadcasted_iota(jnp.int32, logits.shape, 1)
    neg = jnp.float32(-jnp.inf)
    cur = jnp.where(lane < ne, logits, neg)
    vals, idxs = [], []
    for _ in range(TOP_K):
        mx = jnp.max(cur, axis=-1, keepdims=True)
        ix = jnp.min(jnp.where(cur == mx, lane, LANE), axis=-1, keepdims=True)
        vals.append(mx)
        idxs.append(ix)
        cur = jnp.where(lane == ix, neg, cur)
    es = [jnp.exp(vk - vals[0]) for vk in vals]
    den = es[0]
    for ek in es[1:]:
        den = den + ek
    ti = jnp.zeros(logits.shape, jnp.int32)
    tw = jnp.zeros(logits.shape, F32)
    for kq in range(TOP_K):
        ti = jnp.where(lane == kq, idxs[kq], ti)
        tw = jnp.where(lane == kq, es[kq] / den, tw)
    ti_ref[...] = ti
    tw_ref[...] = tw


def _out_and_route(m, x2, mod3, g2, wo, rw_hi, rw_lo, rb, seq, ne):
    t, d = x2.shape
    tm = _tile(seq, 256)
    bidx = lambda i: (i * tm) // seq
    modv = lambda which: pl.BlockSpec((None, 1, d), lambda i: (bidx(i), 0, which))
    kern = functools.partial(_mixb_kernel, ne=ne)
    return pl.pallas_call(
        kern,
        grid=(t // tm,),
        in_specs=[pl.BlockSpec((tm, d), lambda i: (i, 0)),
                  pl.BlockSpec((tm, d), lambda i: (i, 0)),
                  modv(2), modv(3), modv(4),
                  pl.BlockSpec((1, d), lambda i: (0, 0)),
                  pl.BlockSpec((d, d), lambda i: (0, 0)),
                  pl.BlockSpec((d, LANE), lambda i: (0, 0)),
                  pl.BlockSpec((d, LANE), lambda i: (0, 0)),
                  pl.BlockSpec((1, LANE), lambda i: (0, 0))],
        out_specs=[pl.BlockSpec((tm, d), lambda i: (i, 0)),
                   pl.BlockSpec((tm, d), lambda i: (i, 0)),
                   pl.BlockSpec((tm, LANE), lambda i: (i, 0)),
                   pl.BlockSpec((tm, LANE), lambda i: (i, 0))],
        out_shape=[jax.ShapeDtypeStruct((t, d), F32), jax.ShapeDtypeStruct((t, d), F32),
                   jax.ShapeDtypeStruct((t, LANE), jnp.int32), jax.ShapeDtypeStruct((t, LANE), F32)],
        compiler_params=_params("parallel"),
        name="out_and_route",
    )(m, x2, mod3, mod3, mod3, g2.reshape(1, d), wo, rw_hi, rw_lo, rb)


def _expert_kernel(be_ref, nv_ref, src_ref, dst_ref, h_hbm, wgu_ref, bgu_ref, wd_ref, bd_ref, o_hbm,
                   xbuf, ybuf, gsem, ssem, *, bm, de):
    del be_ref
    i = pl.program_id(0)
    nv = nv_ref[i]

    @pl.when(nv > 0)
    def _():
        def gather(rw):
            return pltpu.make_async_copy(h_hbm.at[pl.ds(src_ref[0, rw], 1)], xbuf.at[pl.ds(rw, 1)], gsem)

        def scatter(rw):
            return pltpu.make_async_copy(ybuf.at[pl.ds(rw, 1)], o_hbm.at[pl.ds(dst_ref[0, rw], 1)], ssem)

        def start_g(rw, cr):
            gather(rw).start()
            return cr

        def wait_g(rw, cr):
            gather(rw).wait()
            return cr

        lax.fori_loop(0, bm, start_g, 0)
        lax.fori_loop(0, bm, wait_g, 0)
        x = xbuf[...].astype(BF)
        gu = _dot(x, wgu_ref[...]) + bgu_ref[...]
        gate = jnp.minimum(gu[:, :de], SWIGLU_LIMIT)
        up = jnp.clip(gu[:, de:], -SWIGLU_LIMIT, SWIGLU_LIMIT)
        act = ((up + 1.0) * gate * _sigmoid(SWIGLU_ALPHA * gate)).astype(BF)
        ybuf[...] = _dot(act, wd_ref[...]) + bd_ref[...]

        def start_s(rw, cr):
            scatter(rw).start()
            return cr

        def wait_s(rw, cr):
            scatter(rw).wait()
            return cr

        lax.fori_loop(0, nv, start_s, 0)
        lax.fori_loop(0, nv, wait_s, 0)


def _experts(h, block_expert, nvalid, src_tok, dst_row, wgu, bgu, wd, bd, bm):
    t, d = h.shape
    ne, _, de2 = wgu.shape
    de = de2 // 2
    nb = block_expert.shape[0]
    m = t * TOP_K
    kern = functools.partial(_expert_kernel, bm=bm, de=de)
    smem_row = pl.BlockSpec((None, 1, bm), lambda i, be, nv: (i, 0, 0), memory_space=pltpu.SMEM)
    grid_spec = pltpu.PrefetchScalarGridSpec(
        num_scalar_prefetch=2,
        grid=(nb,),
        in_specs=[smem_row, smem_row,
                  pl.BlockSpec(memory_space=pl.ANY),
                  pl.BlockSpec((None, d, de2), lambda i, be, nv: (be[i], 0, 0)),
                  pl.BlockSpec((None, 1, de2), lambda i, be, nv: (be[i], 0, 0)),
                  pl.BlockSpec((None, de, d), lambda i, be, nv: (be[i], 0, 0)),
                  pl.BlockSpec((None, 1, d), lambda i, be, nv: (be[i], 0, 0))],
        out_specs=pl.BlockSpec(memory_space=pl.ANY),
        scratch_shapes=[pltpu.VMEM((bm, d), F32), pltpu.VMEM((bm, d), F32),
                        pltpu.SemaphoreType.DMA, pltpu.SemaphoreType.DMA],
    )
    return pl.pallas_call(
        kern,
        grid_spec=grid_spec,
        out_shape=jax.ShapeDtypeStruct((m, d), F32),
        compiler_params=_params("arbitrary"),
        name="experts",
    )(block_expert, nvalid, src_tok.reshape(nb, 1, bm), dst_row.reshape(nb, 1, bm), h,
      wgu, bgu.reshape(ne, 1, de2), wd, bd.reshape(ne, 1, d))


def _routing_tables(top_idx, ne, bm):
    t = top_idx.shape[0]
    m = t * TOP_K
    flat_e = top_idx.reshape(m)
    order = jnp.argsort(flat_e).astype(jnp.int32)
    counts = jnp.sum((flat_e[:, None] == jnp.arange(ne, dtype=jnp.int32)[None, :]).astype(jnp.int32), axis=0)
    padded = (counts + bm - 1) // bm * bm
    starts = jnp.cumsum(counts) - counts
    pad_ends = jnp.cumsum(padded)
    pad_starts = pad_ends - padded
    nb = -(-m // bm) + ne
    block_start = jnp.arange(nb, dtype=jnp.int32) * bm
    block_expert = jnp.minimum(jnp.searchsorted(pad_ends, block_start, side='right'), ne - 1).astype(jnp.int32)
    rank0 = block_start - pad_starts[block_expert]
    nvalid = jnp.clip(counts[block_expert] - rank0, 0, bm).astype(jnp.int32)
    rank = rank0[:, None] + jnp.arange(bm, dtype=jnp.int32)[None, :]
    valid = jnp.arange(bm, dtype=jnp.int32)[None, :] < nvalid[:, None]
    pos = jnp.clip(starts[block_expert][:, None] + rank, 0, m - 1)
    assign = order[pos]
    src_tok = jnp.where(valid, assign // TOP_K, 0).astype(jnp.int32)
    dst_row = jnp.where(valid, assign, 0).astype(jnp.int32)
    return block_expert, nvalid, src_tok, dst_row


def _combine_kernel(x_ref, y_ref, tw_ref, gt_ref, fg_ref, o_ref, *, d, final):
    tw = tw_ref[...]
    acc = tw[:, 0:1] * y_ref[:, 0:d]
    for kq in range(1, TOP_K):
        acc = acc + tw[:, kq:kq + 1] * y_ref[:, kq * d:(kq + 1) * d]
    x = x_ref[...] + gt_ref[...] * acc
    if final:
        ms = jnp.mean(x * x, axis=-1, keepdims=True)
        x = x * lax.rsqrt(ms + RMS_EPS) * fg_ref[...]
    o_ref[...] = x


def _combine(x2, y_rows, top_w, mod3, final_g, seq, final):
    t, d = x2.shape
    tm = _tile(seq, 256)
    bidx = lambda i: (i * tm) // seq
    kern = functools.partial(_combine_kernel, d=d, final=final)
    return pl.pallas_call(
        kern,
        grid=(t // tm,),
        in_specs=[pl.BlockSpec((tm, d), lambda i: (i, 0)),
                  pl.BlockSpec((tm, TOP_K * d), lambda i: (i, 0)),
                  pl.BlockSpec((tm, LANE), lambda i: (i, 0)),
                  pl.BlockSpec((None, 1, d), lambda i: (bidx(i), 0, 5)),
                  pl.BlockSpec((1, d), lambda i: (0, 0))],
        out_specs=pl.BlockSpec((tm, d), lambda i: (i, 0)),
        out_shape=jax.ShapeDtypeStruct((t, d), F32),
        compiler_params=_params("parallel"),
        name="combine",
    )(x2, y_rows.reshape(t, TOP_K * d), top_w, mod3, final_g.reshape(1, d))


def kernel(x, c, norm1_g, norm2_g, w_mod, b_mod, w_in, conv_w, conv_b, conv_ln_g, conv_ln_b, w_conv_proj, mu_shift, w0, w2, a0, a2, g2, k_k, k_a, r_k, gn_g, gn_b, w_rwkv_proj, v0, v1, mu_v, v2, w_out, router_w, router_b, w_gate_up, b_gate_up, w_down, b_down, final_g):
    bsz, seq, d = x.shape
    depth = w_mod.shape[0]
    ca = conv_w.shape[-1]
    cb = w0.shape[1]
    r_decay, r_aaa, r_gate, r_mv = w2.shape[1], a2.shape[1], g2.shape[1], v1.shape[2]
    ne = router_w.shape[2]
    assert r_decay + r_aaa + r_mv == LORA_W and r_gate == LORA_W
    assert (2 * ca) % cb == 0 and cb == d and ne <= LANE
    t = bsz * seq
    c0 = 2 * ca
    c_lora = c0 + 3 * cb
    c_gate = c_lora + r_decay + r_aaa + r_gate
    col_gate = c0 + 3 * cb
    col_lora = col_gate + 2 * d
    bm = 256

    mod = _modulation(c, w_mod, b_mod)
    x2 = x.reshape(t, d)
    v_first = None
    for l in range(depth):
        mod3 = mod[l].reshape(bsz, 1, 6 * d)
        wl = w_in[l]
        if l == 0:
            v1_cols = jnp.zeros((d, r_mv), F32)
            mu_mv = jnp.zeros((r_mv,), F32)
        else:
            v1_cols = v1[l - 1]
            mu_mv = mu_v[l - 1]
        w_all = jnp.concatenate(
            [wl[:, :c_lora], wl[:, c_gate:], wl[:, c_lora:c_lora + r_decay + r_aaa], v1_cols,
             wl[:, c_lora + r_decay + r_aaa:c_gate]], axis=1).astype(BF)
        mu_l = mu_shift[l]
        mu_lora = jnp.concatenate([mu_l[3 * cb:3 * cb + r_decay + r_aaa], mu_mv, mu_l[3 * cb + r_decay + r_aaa:]])
        zpad = lambda top, w, bot: jnp.concatenate(
            [jnp.zeros((top, cb), F32), w, jnp.zeros((bot, cb), F32)], axis=0).astype(BF)
        w2p = zpad(0, w2[l], LORA_W - r_decay)
        a2p = zpad(r_decay, a2[l], r_mv)

        proj = _inproj(x2, mod3, norm1_g[l], w_all, seq)
        a_conv = _conv_branch(proj, conv_w[l], conv_b[l], conv_ln_g[l], conv_ln_b[l], seq)
        vres = None if l == 0 else (zpad(r_decay + r_aaa, v2[l - 1], 0), v0[l - 1], v_first)
        r, ld, k, v, kk, a, g = _rwkv_prep(proj, mu_l[:3 * cb], mu_lora, w2p, a2p, g2[l].astype(BF), w0[l], a0[l],
                                           k_k[l], k_a[l], vres, seq, cb, c0, col_lora, r_decay)
        if l == 0:
            v_first = v
        yg = _wkv_scan(r, ld, k, v, kk, a, g, r_k[l], gn_g[l], gn_b[l], bsz, seq)
        mixed = _mix_branches(a_conv, yg, proj, w_conv_proj[l].astype(BF), w_rwkv_proj[l].astype(BF), col_gate)

        rw = jnp.pad(router_w[l], ((0, 0), (0, LANE - ne)))
        rw_hi = rw.astype(BF)
        rw_lo = (rw - rw_hi.astype(F32)).astype(BF)
        rb = jnp.pad(router_b[l], (0, LANE - ne)).reshape(1, LANE)
        x2, h, top_i, top_w = _out_and_route(mixed, x2, mod3, norm2_g[l], w_out[l].astype(BF), rw_hi, rw_lo, rb,
                                             seq, ne)
        tables = _routing_tables(top_i[:, :TOP_K], ne, bm)
        y_rows = _experts(h, *tables, w_gate_up[l].astype(BF), b_gate_up[l], w_down[l].astype(BF), b_down[l], bm)
        x2 = _combine(x2, y_rows, top_w, mod3, final_g, seq, final=(l == depth - 1))
    return x2.reshape(bsz, seq, d)
```

```python
import functools

import jax
import jax.numpy as jnp
from jax import lax
from jax.experimental import pallas as pl
from jax.experimental.pallas import tpu as pltpu

F32 = jnp.float32
BF = jnp.bfloat16

HEAD = 64
CHUNK = 64
SCAN_LANES = 4 * HEAD
SCAN_GROUPS = 4
TOP_K = 4
RMS_EPS = 1e-5
LN_EPS = 1e-5
GN_EPS = 64e-5
SWIGLU_LIMIT = 7.0
SWIGLU_ALPHA = 1.702
LORA_W = 256
LANE = 128
CONV_HALO = 32
SHIFT_HALO = 16
VMEM_LIMIT = 56 * 1024 * 1024


def _params(*sem):
    return pltpu.CompilerParams(dimension_semantics=sem, vmem_limit_bytes=VMEM_LIMIT)


def _tile(n, pref):
    t = min(n, pref)
    while n % t:
        t -= 8
    return t


def _split2(x):
    hi = x.astype(BF)
    lo = (x - hi.astype(F32)).astype(BF)
    return hi, lo


def _split3(x):
    hi = x.astype(BF)
    r1 = x - hi.astype(F32)
    mid = r1.astype(BF)
    lo = (r1 - mid.astype(F32)).astype(BF)
    return hi, mid, lo


def _dot(a, b):
    return jnp.dot(a, b, preferred_element_type=F32)


def _dot_nt(a, b):
    return lax.dot_general(a, b, (((1,), (1,)), ((), ())), preferred_element_type=F32)


def _sigmoid(x):
    return 1.0 / (1.0 + jnp.exp(-x))


def _mod_kernel(c_ref, w_ref, b_ref, o_ref):
    c = c_ref[...]
    cond = (c * _sigmoid(c)).astype(BF)
    o_ref[...] = _dot(cond, w_ref[...].astype(BF)) + b_ref[...]


def _modulation(c, w_mod, b_mod):
    nl, d, n6 = w_mod.shape
    b = c.shape[0]
    tn = _tile(n6, 1024)
    return pl.pallas_call(
        _mod_kernel,
        grid=(nl, n6 // tn),
        in_specs=[pl.BlockSpec((b, d), lambda l, j: (0, 0)),
                  pl.BlockSpec((None, d, tn), lambda l, j: (l, 0, j)),
                  pl.BlockSpec((None, 1, tn), lambda l, j: (l, 0, j))],
        out_specs=pl.BlockSpec((None, b, tn), lambda l, j: (l, 0, j)),
        out_shape=jax.ShapeDtypeStruct((nl, b, n6), F32),
        compiler_params=_params("parallel", "parallel"),
        name="modulation",
    )(c, w_mod, b_mod.reshape(nl, 1, n6))


def _inproj_kernel(x_ref, sh_ref, sc_ref, g_ref, w_ref, o_ref, u_ref):
    @pl.when(pl.program_id(1) == 0)
    def _():
        x = x_ref[...]
        ms = jnp.mean(x * x, axis=-1, keepdims=True)
        xn = x * lax.rsqrt(ms + RMS_EPS) * g_ref[...]
        u_ref[...] = (xn * (1.0 + sc_ref[...]) + sh_ref[...]).astype(BF)

    o_ref[...] = _dot(u_ref[...], w_ref[...]).astype(o_ref.dtype)


def _inproj(x2, mod3, g, w, seq):
    t, d = x2.shape
    n = w.shape[1]
    tm = _tile(seq, 1024)
    tn = _tile(n, 512)
    bidx = lambda i: (i * tm) // seq
    return pl.pallas_call(
        _inproj_kernel,
        grid=(t // tm, n // tn),
        in_specs=[pl.BlockSpec((tm, d), lambda i, j: (i, 0)),
                  pl.BlockSpec((None, 1, d), lambda i, j: (bidx(i), 0, 0)),
                  pl.BlockSpec((None, 1, d), lambda i, j: (bidx(i), 0, 1)),
                  pl.BlockSpec((1, d), lambda i, j: (0, 0)),
                  pl.BlockSpec((d, tn), lambda i, j: (0, j))],
        out_specs=pl.BlockSpec((tm, tn), lambda i, j: (i, j)),
        out_shape=jax.ShapeDtypeStruct((t, n), BF),
        scratch_shapes=[pltpu.VMEM((tm, d), BF)],
        compiler_params=_params("parallel", "arbitrary"),
        name="inproj",
    )(x2, mod3, mod3, g.reshape(1, d), w)


def _conv_kernel(p_ref, h_ref, w_ref, cb_ref, g_ref, b_ref, o_ref, a_ref, y_ref, *, ts, ca, taps, seq):
    i = pl.program_id(0)
    p = p_ref[...].astype(F32)
    a_ref[pl.ds(CONV_HALO, ts), :] = p[:, :ca] * _sigmoid(p[:, ca:])
    hp = h_ref[...].astype(F32)
    keep = jnp.where((i * ts) % seq == 0, 0.0, 1.0)
    a_ref[pl.ds(0, CONV_HALO), :] = hp[:, :ca] * _sigmoid(hp[:, ca:]) * keep

    rb = min(ts, 64)
    off = CONV_HALO - (taps - 1)

    for r0 in range(0, ts, rb):
        for c0 in range(0, ca, LANE):
            acc = jnp.zeros((rb, LANE), F32)
            for j in range(taps):
                acc = acc + a_ref[r0 + off + j:r0 + off + j + rb, c0:c0 + LANE] * w_ref[j:j + 1, c0:c0 + LANE]
            y_ref[r0:r0 + rb, c0:c0 + LANE] = acc
    y = y_ref[...] + cb_ref[...]
    mu = jnp.mean(y, axis=-1, keepdims=True)
    yc = y - mu
    var = jnp.mean(yc * yc, axis=-1, keepdims=True)
    yn = yc * lax.rsqrt(var + LN_EPS) * g_ref[...] + b_ref[...]
    o_ref[...] = (yn * _sigmoid(yn)).astype(o_ref.dtype)


def _conv_branch(proj, conv_w, conv_b, ln_g, ln_b, seq):
    t = proj.shape[0]
    taps, _, ca = conv_w.shape
    assert taps - 1 <= CONV_HALO
    ts = _tile(seq, 256)
    hb = ts // CONV_HALO
    kern = functools.partial(_conv_kernel, ts=ts, ca=ca, taps=taps, seq=seq)
    return pl.pallas_call(
        kern,
        grid=(t // ts,),
        in_specs=[pl.BlockSpec((ts, 2 * ca), lambda i: (i, 0)),
                  pl.BlockSpec((CONV_HALO, 2 * ca), lambda i: (jnp.maximum(i * hb - 1, 0), 0)),
                  pl.BlockSpec((taps, ca), lambda i: (0, 0)),
                  pl.BlockSpec((1, ca), lambda i: (0, 0)),
                  pl.BlockSpec((1, ca), lambda i: (0, 0)),
                  pl.BlockSpec((1, ca), lambda i: (0, 0))],
        out_specs=pl.BlockSpec((ts, ca), lambda i: (i, 0)),
        out_shape=jax.ShapeDtypeStruct((t, ca), BF),
        scratch_shapes=[pltpu.VMEM((ts + CONV_HALO, ca), F32), pltpu.VMEM((ts, ca), F32)],
        compiler_params=_params("parallel"),
        name="conv_branch",
    )(proj, proj, conv_w.reshape(taps, ca), conv_b.reshape(1, ca), ln_g.reshape(1, ca), ln_b.reshape(1, ca))


def _token_shift(cur_ref, halo_ref, mu, first):
    cur = cur_ref[...].astype(F32)
    last = halo_ref[SHIFT_HALO - 1:SHIFT_HALO, :].astype(F32)
    last = jnp.where(first, 0.0, last)
    row = lax.broadcasted_iota(jnp.int32, cur.shape, 0)
    prev = jnp.where(row == 0, last, pltpu.roll(cur, 1, axis=0))
    return cur + (prev - cur) * mu


def _prep_kernel(*refs, ts, seq, cb, r_decay, has_vres):
    (r_ref, k_ref, v_ref, l_ref, rh_ref, kh_ref, vh_ref, lh_ref, mur_ref, muk_ref, muv_ref, mul_ref,
     w2_ref, a2_ref, g2_ref, w0_ref, a0_ref, kk_ref, ka_ref, e_ref, et_ref) = refs[:21]
    rest = refs[21:]
    if has_vres:
        v2_ref, v0_ref, vf_ref = rest[:3]
        rest = rest[3:]
    ro_ref, ldo_ref, ko_ref, vo_ref, kko_ref, ao_ref, go_ref = rest

    first = (pl.program_id(0) * ts) % seq == 0
    r = _token_shift(r_ref, rh_ref, mur_ref[...], first)
    k = _token_shift(k_ref, kh_ref, muk_ref[...], first)
    v = _token_shift(v_ref, vh_ref, muv_ref[...], first)
    lz = _token_shift(l_ref, lh_ref, mul_ref[...], first)

    l1 = lz[:, :LORA_W]
    lane = lax.broadcasted_iota(jnp.int32, l1.shape, 1)
    t1 = jnp.where(lane < r_decay, jnp.tanh(l1), l1).astype(BF)
    wpre = w0_ref[...] + _dot(t1, w2_ref[...])
    w = -(jnp.maximum(-wpre, 0.0) + jnp.log(1.0 + jnp.exp(-jnp.abs(wpre)))) - 0.5
    ldo_ref[...] = -jnp.exp(w)
    a = _sigmoid(a0_ref[...] + _dot(t1, a2_ref[...]))
    go_ref[...] = _dot(_sigmoid(lz[:, LORA_W:]).astype(BF), g2_ref[...]).astype(go_ref.dtype)
    if has_vres:
        v = v + (vf_ref[...] - v) * _sigmoid(v0_ref[...] + _dot(t1, v2_ref[...]))

    kk = k * kk_ref[...]
    sq_hi, sq_lo = _split2(kk * kk)
    ss = _dot(sq_hi, e_ref[...]) + _dot(sq_lo, e_ref[...])
    inv = lax.rsqrt(jnp.maximum(ss, 1e-24))
    inv_hi, inv_mid, inv_lo = _split3(inv)
    invb = _dot(inv_hi, et_ref[...]) + _dot(inv_mid, et_ref[...]) + _dot(inv_lo, et_ref[...])
    ro_ref[...] = r
    ko_ref[...] = k * (1.0 + (a - 1.0) * ka_ref[...])
    vo_ref[...] = v
    kko_ref[...] = kk * invb
    ao_ref[...] = a


def _rwkv_prep(proj, mu_rkv, mu_lora, w2p, a2p, g2, w0, a0, k_k, k_a, vres, seq, cb, col_r, col_lora, r_decay):
    t = proj.shape[0]
    ts = _tile(seq, 256)
    hb = ts // SHIFT_HALO
    nh = cb // HEAD
    hp = max(nh, LANE)
    head_of = jnp.arange(cb, dtype=jnp.int32) // HEAD
    e = (head_of[:, None] == jnp.arange(hp, dtype=jnp.int32)[None, :]).astype(BF)
    et = e.T
    lw = 2 * LORA_W
    cr = col_r // cb
    cl = col_lora // lw
    halo = lambda i: jnp.maximum(i * hb - 1, 0)
    row = lambda n: pl.BlockSpec((1, n), lambda i: (0, 0))
    full = lambda a, b: pl.BlockSpec((a, b), lambda i: (0, 0))
    in_specs = [pl.BlockSpec((ts, cb), lambda i: (i, cr)),
                pl.BlockSpec((ts, cb), lambda i: (i, cr + 1)),
                pl.BlockSpec((ts, cb), lambda i: (i, cr + 2)),
                pl.BlockSpec((ts, lw), lambda i: (i, cl)),
                pl.BlockSpec((SHIFT_HALO, cb), lambda i: (halo(i), cr)),
                pl.BlockSpec((SHIFT_HALO, cb), lambda i: (halo(i), cr + 1)),
                pl.BlockSpec((SHIFT_HALO, cb), lambda i: (halo(i), cr + 2)),
                pl.BlockSpec((SHIFT_HALO, lw), lambda i: (halo(i), cl)),
                row(cb), row(cb), row(cb), row(lw),
                full(LORA_W, cb), full(LORA_W, cb), full(LORA_W, cb),
                row(cb), row(cb), row(cb), row(cb), full(cb, hp), full(hp, cb)]
    args = [proj, proj, proj, proj, proj, proj, proj, proj,
            mu_rkv[0:cb].reshape(1, cb), mu_rkv[cb:2 * cb].reshape(1, cb), mu_rkv[2 * cb:3 * cb].reshape(1, cb),
            mu_lora.reshape(1, lw), w2p, a2p, g2, w0.reshape(1, cb), a0.reshape(1, cb),
            k_k.reshape(1, cb), k_a.reshape(1, cb), e, et]
    if vres is not None:
        v2p, v0, v_first = vres
        in_specs += [full(LORA_W, cb), row(cb), pl.BlockSpec((ts, cb), lambda i: (i, 0))]
        args += [v2p, v0.reshape(1, cb), v_first]
    tile = pl.BlockSpec((ts, cb), lambda i: (i, 0))
    kern = functools.partial(_prep_kernel, ts=ts, seq=seq, cb=cb, r_decay=r_decay, has_vres=vres is not None)
    f32o = jax.ShapeDtypeStruct((t, cb), F32)
    return pl.pallas_call(
        kern,
        grid=(t // ts,),
        in_specs=in_specs,
        out_specs=[tile] * 7,
        out_shape=[f32o] * 6 + [jax.ShapeDtypeStruct((t, cb), BF)],
        compiler_params=_params("parallel"),
        name="rwkv_prep",
    )(*args)


def _scan_kernel(r_ref, ld_ref, k_ref, v_ref, kk_ref, a_ref, g_ref, rk_ref, gg_ref, gb_ref, o_ref, s_ref,
                 *, nchunk, ng):
    c = CHUNK
    lg = SCAN_LANES
    nq = lg // HEAD
    groups = range(ng)

    @pl.when(pl.program_id(2) == 0)
    def _():
        s_ref[...] = jnp.zeros_like(s_ref)

    row = lax.broadcasted_iota(jnp.int32, (lg, lg), 0)
    col = lax.broadcasted_iota(jnp.int32, (lg, lg), 1)
    same = (row // c) == (col // HEAD)
    bd = same.astype(F32)
    strict = same & (row > col)
    incl = same & (row >= col)
    eye = (row == col).astype(F32)
    avg = (bd * (1.0 / HEAD)).astype(BF)
    ones = bd.astype(BF)
    tr = lax.broadcasted_iota(jnp.int32, (c, c), 0)
    tc = lax.broadcasted_iota(jnp.int32, (c, c), 1)
    ltri = (tr >= tc).astype(BF)

    def stack(x):
        return jnp.concatenate([x] * nq, axis=0) * bd

    def unstack(y):
        out = y[0:c]
        for qi in range(1, nq):
            out = out + y[qi * c:(qi + 1) * c]
        return out

    def body(ci, carry):
        sl = pl.ds(pl.multiple_of(ci * c, c), c)
        ld = ld_ref[sl, :]
        l_hi, l_mid, l_lo = _split3(ld)
        cum = _dot(ltri, l_hi) + _dot(ltri, l_mid) + _dot(ltri, l_lo)
        dinc = jnp.exp(cum)
        dinv = jnp.exp(-cum)
        dexc = jnp.exp(cum - ld)
        kk = kk_ref[sl, :]
        r = r_ref[sl, :]
        k = k_ref[sl, :]
        v = v_ref[sl, :]
        at = -kk * dexc
        rt = r * dinc
        bt = kk * a_ref[sl, :] * dinv
        kt = k * dinv
        dtot = dinc[c - 1:c, :]
        lanes = [slice(gi * lg, (gi + 1) * lg) for gi in groups]

        r_s = [stack(rt[:, ln]) for ln in lanes]
        b_s = [stack(bt[:, ln]) for ln in lanes]
        k_s = [stack(kt[:, ln]) for ln in lanes]
        a_b = [stack(at[:, ln]).astype(BF) for ln in lanes]
        v_b = [stack(v[:, ln]).astype(BF) for ln in lanes]
        r_b = [x.astype(BF) for x in r_s]
        b_b = [x.astype(BF) for x in b_s]
        k_b = [x.astype(BF) for x in k_s]

        a_ab = [jnp.where(strict, _dot_nt(a_b[gi], b_b[gi]), 0.0) for gi in groups]
        a_ak = [jnp.where(strict, _dot_nt(a_b[gi], k_b[gi]), 0.0).astype(BF) for gi in groups]
        a_rb = [jnp.where(incl, _dot_nt(r_b[gi], b_b[gi]), 0.0).astype(BF) for gi in groups]
        a_rk = [jnp.where(incl, _dot_nt(r_b[gi], k_b[gi]), 0.0).astype(BF) for gi in groups]

        tm = [eye + x for x in a_ab]
        pw = a_ab
        for _ in range(c.bit_length() - 2):
            pb = [x.astype(BF) for x in pw]
            pw = [_dot(x, x) for x in pb]
            tm = [tm[gi] + _dot(tm[gi].astype(BF), pw[gi].astype(BF)) for gi in groups]
        tmb = [x.astype(BF) for x in tm]

        akv = [_dot(a_ak[gi], v_b[gi]).astype(BF) for gi in groups]
        wb = [_dot(tmb[gi], a_b[gi]).astype(BF) for gi in groups]
        u0b = [_dot(tmb[gi], akv[gi]).astype(BF) for gi in groups]
        q = [unstack(r_s[gi] + _dot(a_rb[gi], wb[gi])).astype(BF) for gi in groups]
        y0 = [unstack(_dot(a_rb[gi], u0b[gi]) + _dot(a_rk[gi], v_b[gi])) for gi in groups]
        bd_t = [(b_s[gi] * dtot[:, lanes[gi]]).T.astype(BF) for gi in groups]
        kd_t = [(k_s[gi] * dtot[:, lanes[gi]]).T.astype(BF) for gi in groups]
        m = [(eye * dtot[:, lanes[gi]] + _dot(bd_t[gi], wb[gi])).astype(BF) for gi in groups]
        pm = [_dot(bd_t[gi], u0b[gi]) + _dot(kd_t[gi], v_b[gi]) for gi in groups]

        s0 = [s_ref[gi].astype(BF) for gi in groups]
        yt = [_dot(q[gi], s0[gi]) + y0[gi] for gi in groups]
        for gi in groups:
            s_ref[gi] = _dot(m[gi], s0[gi]) + pm[gi]

        for gi in groups:
            ln = lanes[gi]
            y_hi, y_lo = _split2(yt[gi])
            mu = _dot(y_hi, avg) + _dot(y_lo, avg)
            yc = yt[gi] - mu
            s_hi, s_lo = _split2(yc * yc)
            var = _dot(s_hi, avg) + _dot(s_lo, avg)
            yn = yc * lax.rsqrt(var + GN_EPS) * gg_ref[:, ln] + gb_ref[:, ln]
            b_hi, b_lo = _split2(r[:, ln] * k[:, ln] * rk_ref[:, ln])
            bonus = _dot(b_hi, ones) + _dot(b_lo, ones)
            o_ref[sl, ln] = ((yn + bonus * v[:, ln]) * g_ref[sl, ln].astype(F32)).astype(o_ref.dtype)
        return carry

    lax.fori_loop(0, nchunk, body, 0)


def _wkv_scan(r, ld, k, v, kk, a, g, r_k, gn_g, gn_b, bsz, seq):
    t, cb = r.shape
    assert cb % SCAN_LANES == 0
    ng = min(cb // SCAN_LANES, SCAN_GROUPS)
    lb = ng * SCAN_LANES
    sb = _tile(seq, 256)
    nsb = seq // sb
    tile = pl.BlockSpec((sb, lb), lambda b, h, s: (b * nsb + s, h))
    vec = pl.BlockSpec((1, lb), lambda b, h, s: (0, h))
    kern = functools.partial(_scan_kernel, nchunk=sb // CHUNK, ng=ng)
    return pl.pallas_call(
        kern,
        grid=(bsz, cb // lb, nsb),
        in_specs=[tile] * 7 + [vec] * 3,
        out_specs=tile,
        out_shape=jax.ShapeDtypeStruct((t, cb), BF),
        scratch_shapes=[pltpu.VMEM((ng, SCAN_LANES, SCAN_LANES), F32)],
        compiler_params=_params("parallel", "parallel", "arbitrary"),
        name="wkv_scan",
    )(r, ld, k, v, kk, a, g, r_k.reshape(1, cb), gn_g.reshape(1, cb), gn_b.reshape(1, cb))


def _mixa_kernel(a_ref, y_ref, ga_ref, gb_ref, wc_ref, wr_ref, o_ref):
    ya = _dot(a_ref[...], wc_ref[...])
    yb = _dot(y_ref[...], wr_ref[...])
    m = _sigmoid(ga_ref[...].astype(F32)) * ya + _sigmoid(gb_ref[...].astype(F32)) * yb
    o_ref[...] = m.astype(o_ref.dtype)


def _mix_branches(a_conv, yg, proj, wc, wr, col_gate):
    t, ca = a_conv.shape
    cb = yg.shape[1]
    d = wc.shape[1]
    tm = _tile(t, 512)
    cg = col_gate // d
    return pl.pallas_call(
        _mixa_kernel,
        grid=(t // tm,),
        in_specs=[pl.BlockSpec((tm, ca), lambda i: (i, 0)),
                  pl.BlockSpec((tm, cb), lambda i: (i, 0)),
                  pl.BlockSpec((tm, d), lambda i: (i, cg)),
                  pl.BlockSpec((tm, d), lambda i: (i, cg + 1)),
                  pl.BlockSpec((ca, d), lambda i: (0, 0)),
                  pl.BlockSpec((cb, d), lambda i: (0, 0))],
        out_specs=pl.BlockSpec((tm, d), lambda i: (i, 0)),
        out_shape=jax.ShapeDtypeStruct((t, d), BF),
        compiler_params=_params("parallel"),
        name="mix_branches",
    )(a_conv, yg, proj, proj, wc, wr)


def _mixb_kernel(m_ref, x_ref, gt_ref, sh_ref, sc_ref, g_ref, wo_ref, rwh_ref, rwl_ref, rb_ref,
                 xo_ref, ho_ref, ti_ref, tw_ref, *, ne):
    x = x_ref[...] + gt_ref[...] * _dot(m_ref[...], wo_ref[...])
    xo_ref[...] = x
    ms = jnp.mean(x * x, axis=-1, keepdims=True)
    h = x * lax.rsqrt(ms + RMS_EPS) * g_ref[...] * (1.0 + sc_ref[...]) + sh_ref[...]
    ho_ref[...] = h
    h_hi, h_lo = _split2(h)
    logits = _dot(h_hi, rwh_ref[...]) + _dot(h_lo, rwh_ref[...]) + _dot(h_hi, rwl_ref[...]) + rb_ref[...]
    lane = lax.broadcasted_iota(jnp.int32, logits.shape, 1)
    neg = jnp.float32(-jnp.inf)
    cur = jnp.where(lane < ne, logits, neg)
    vals, idxs = [], []
    for _ in range(TOP_K):
        mx = jnp.max(cur, axis=-1, keepdims=True)
        ix = jnp.min(jnp.where(cur == mx, lane, LANE), axis=-1, keepdims=True)
        vals.append(mx)
        idxs.append(ix)
        cur = jnp.where(lane == ix, neg, cur)
    es = [jnp.exp(vk - vals[0]) for vk in vals]
    den = es[0]
    for ek in es[1:]:
        den = den + ek
    ti = jnp.zeros(logits.shape, jnp.int32)
    tw = jnp.zeros(logits.shape, F32)
    for kq in range(TOP_K):
        ti = jnp.where(lane == kq, idxs[kq], ti)
        tw = jnp.where(lane == kq, es[kq] / den, tw)
    ti_ref[...] = ti
    tw_ref[...] = tw


def _out_and_route(m, x2, mod3, g2, wo, rw_hi, rw_lo, rb, seq, ne):
    t, d = x2.shape
    tm = _tile(seq, 256)
    bidx = lambda i: (i * tm) // seq
    modv = lambda which: pl.BlockSpec((None, 1, d), lambda i: (bidx(i), 0, which))
    kern = functools.partial(_mixb_kernel, ne=ne)
    return pl.pallas_call(
        kern,
        grid=(t // tm,),
        in_specs=[pl.BlockSpec((tm, d), lambda i: (i, 0)),
                  pl.BlockSpec((tm, d), lambda i: (i, 0)),
                  modv(2), modv(3), modv(4),
                  pl.BlockSpec((1, d), lambda i: (0, 0)),
                  pl.BlockSpec((d, d), lambda i: (0, 0)),
                  pl.BlockSpec((d, LANE), lambda i: (0, 0)),
                  pl.BlockSpec((d, LANE), lambda i: (0, 0)),
                  pl.BlockSpec((1, LANE), lambda i: (0, 0))],
        out_specs=[pl.BlockSpec((tm, d), lambda i: (i, 0)),
                   pl.BlockSpec((tm, d), lambda i: (i, 0)),
                   pl.BlockSpec((tm, LANE), lambda i: (i, 0)),
                   pl.BlockSpec((tm, LANE), lambda i: (i, 0))],
        out_shape=[jax.ShapeDtypeStruct((t, d), F32), jax.ShapeDtypeStruct((t, d), F32),
                   jax.ShapeDtypeStruct((t, LANE), jnp.int32), jax.ShapeDtypeStruct((t, LANE), F32)],
        compiler_params=_params("parallel"),
        name="out_and_route",
    )(m, x2, mod3, mod3, mod3, g2.reshape(1, d), wo, rw_hi, rw_lo, rb)


def _expert_kernel(be_ref, src_cur, src_nxt, dst_prv, dst_cur, h_hbm, wgu_ref, bgu_ref, wd_ref, bd_ref, o_hbm,
                   x0, x1, y0, y1, gsem, ssem, *, bm, de, nb):
    del be_ref
    i = pl.program_id(0)

    def start_gather(src_ref, xb, sem):
        for rw in range(bm):
            pltpu.make_async_copy(h_hbm.at[pl.ds(src_ref[0, rw], 1)], xb.at[pl.ds(rw, 1)], sem).start()

    def wait_gather(xb, sem):
        pltpu.make_async_copy(h_hbm.at[pl.ds(0, bm)], xb, sem).wait()

    def start_scatter(dst_ref, yb, sem):
        for rw in range(bm):
            pltpu.make_async_copy(yb.at[pl.ds(rw, 1)], o_hbm.at[pl.ds(dst_ref[0, rw], 1)], sem).start()

    def wait_scatter(yb, sem):
        pltpu.make_async_copy(yb, o_hbm.at[pl.ds(0, bm)], sem).wait()

    def step(xc, yc, xo, yo, gc, go, sc, so):
        @pl.when(i == 0)
        def _():
            start_gather(src_cur, xc, gc)
            yo[...] = jnp.zeros_like(yo)

        wait_gather(xc, gc)
        start_gather(src_nxt, xo, go)
        start_scatter(dst_prv, yo, so)
        x = xc[...].astype(BF)
        gu = _dot(x, wgu_ref[...]) + bgu_ref[...]
        gate = jnp.minimum(gu[:, :de], SWIGLU_LIMIT)
        up = jnp.clip(gu[:, de:], -SWIGLU_LIMIT, SWIGLU_LIMIT)
        act = ((up + 1.0) * gate * _sigmoid(SWIGLU_ALPHA * gate)).astype(BF)
        yc[...] = _dot(act, wd_ref[...]) + bd_ref[...]
        wait_scatter(yo, so)

        @pl.when(i == nb - 1)
        def _():
            wait_gather(xo, go)
            start_scatter(dst_cur, yc, sc)
            wait_scatter(yc, sc)

    @pl.when(i % 2 == 0)
    def _():
        step(x0, y0, x1, y1, gsem.at[0], gsem.at[1], ssem.at[0], ssem.at[1])

    @pl.when(i % 2 == 1)
    def _():
        step(x1, y1, x0, y0, gsem.at[1], gsem.at[0], ssem.at[1], ssem.at[0])


def _experts(h, block_expert, src_ext, dst_ext, wgu, bgu, wd, bd, bm):
    t, d = h.shape
    ne, _, de2 = wgu.shape
    de = de2 // 2
    nb = block_expert.shape[0]
    m = t * TOP_K
    kern = functools.partial(_expert_kernel, bm=bm, de=de, nb=nb)
    smem_row = lambda off: pl.BlockSpec((None, 1, bm), lambda i, be: (i + off, 0, 0), memory_space=pltpu.SMEM)
    grid_spec = pltpu.PrefetchScalarGridSpec(
        num_scalar_prefetch=1,
        grid=(nb,),
        in_specs=[smem_row(0), smem_row(1), smem_row(0), smem_row(1),
                  pl.BlockSpec(memory_space=pl.ANY),
                  pl.BlockSpec((None, d, de2), lambda i, be: (be[i], 0, 0)),
                  pl.BlockSpec((None, 1, de2), lambda i, be: (be[i], 0, 0)),
                  pl.BlockSpec((None, de, d), lambda i, be: (be[i], 0, 0)),
                  pl.BlockSpec((None, 1, d), lambda i, be: (be[i], 0, 0))],
        out_specs=pl.BlockSpec(memory_space=pl.ANY),
        scratch_shapes=[pltpu.VMEM((bm, d), F32), pltpu.VMEM((bm, d), F32),
                        pltpu.VMEM((bm, d), F32), pltpu.VMEM((bm, d), F32),
                        pltpu.SemaphoreType.DMA((2,)), pltpu.SemaphoreType.DMA((2,))],
    )
    src3 = src_ext.reshape(nb + 1, 1, bm)
    dst3 = dst_ext.reshape(nb + 1, 1, bm)
    return pl.pallas_call(
        kern,
        grid_spec=grid_spec,
        out_shape=jax.ShapeDtypeStruct((m + (nb + 1) * bm, d), F32),
        compiler_params=_params("arbitrary"),
        name="experts",
    )(block_expert, src3, src3, dst3, dst3, h, wgu, bgu.reshape(ne, 1, de2), wd, bd.reshape(ne, 1, d))


def _routing_tables(top_idx, ne, bm):
    t = top_idx.shape[0]
    m = t * TOP_K
    flat_e = top_idx.reshape(m)
    order = jnp.argsort(flat_e).astype(jnp.int32)
    counts = jnp.sum((flat_e[:, None] == jnp.arange(ne, dtype=jnp.int32)[None, :]).astype(jnp.int32), axis=0)
    padded = (counts + bm - 1) // bm * bm
    starts = jnp.cumsum(counts) - counts
    pad_ends = jnp.cumsum(padded)
    pad_starts = pad_ends - padded
    nb = -(-m // bm) + ne
    block_start = jnp.arange(nb, dtype=jnp.int32) * bm
    block_expert = jnp.minimum(jnp.searchsorted(pad_ends, block_start, side='right'), ne - 1).astype(jnp.int32)
    rank = (block_start - pad_starts[block_expert])[:, None] + jnp.arange(bm, dtype=jnp.int32)[None, :]
    valid = (rank >= 0) & (rank < counts[block_expert][:, None])
    assign = order[jnp.clip(starts[block_expert][:, None] + rank, 0, m - 1)]
    src_tok = jnp.where(valid, assign // TOP_K, 0).astype(jnp.int32)
    spare = m + jnp.arange((nb + 1) * bm, dtype=jnp.int32).reshape(nb + 1, bm)
    dst_row = jnp.where(valid, (assign % TOP_K) * t + assign // TOP_K, spare[:nb]).astype(jnp.int32)
    src_ext = jnp.concatenate([src_tok, jnp.zeros((1, bm), jnp.int32)], axis=0)
    dst_ext = jnp.concatenate([spare[nb:], dst_row], axis=0)
    return block_expert, src_ext, dst_ext


def _combine_kernel(x_ref, y0_ref, y1_ref, y2_ref, y3_ref, tw_ref, gt_ref, fg_ref, o_ref, *, final):
    tw = tw_ref[...]
    acc = tw[:, 0:1] * y0_ref[...]
    for kq, y_ref in enumerate((y1_ref, y2_ref, y3_ref), start=1):
        acc = acc + tw[:, kq:kq + 1] * y_ref[...]
    x = x_ref[...] + gt_ref[...] * acc
    if final:
        ms = jnp.mean(x * x, axis=-1, keepdims=True)
        x = x * lax.rsqrt(ms + RMS_EPS) * fg_ref[...]
    o_ref[...] = x


def _combine(x2, y_rows, top_w, mod3, final_g, seq, final):
    t, d = x2.shape
    assert TOP_K == 4
    tm = _tile(seq, 256)
    nt = t // tm
    bidx = lambda i: (i * tm) // seq
    slot = lambda kq: pl.BlockSpec((tm, d), lambda i: (kq * nt + i, 0))
    kern = functools.partial(_combine_kernel, final=final)
    return pl.pallas_call(
        kern,
        grid=(nt,),
        in_specs=[pl.BlockSpec((tm, d), lambda i: (i, 0)),
                  slot(0), slot(1), slot(2), slot(3),
                  pl.BlockSpec((tm, LANE), lambda i: (i, 0)),
                  pl.BlockSpec((None, 1, d), lambda i: (bidx(i), 0, 5)),
                  pl.BlockSpec((1, d), lambda i: (0, 0))],
        out_specs=pl.BlockSpec((tm, d), lambda i: (i, 0)),
        out_shape=jax.ShapeDtypeStruct((t, d), F32),
        compiler_params=_params("parallel"),
        name="combine",
    )(x2, y_rows, y_rows, y_rows, y_rows, top_w, mod3, final_g.reshape(1, d))


def kernel(x, c, norm1_g, norm2_g, w_mod, b_mod, w_in, conv_w, conv_b, conv_ln_g, conv_ln_b, w_conv_proj, mu_shift, w0, w2, a0, a2, g2, k_k, k_a, r_k, gn_g, gn_b, w_rwkv_proj, v0, v1, mu_v, v2, w_out, router_w, router_b, w_gate_up, b_gate_up, w_down, b_down, final_g):
    bsz, seq, d = x.shape
    depth = w_mod.shape[0]
    ca = conv_w.shape[-1]
    cb = w0.shape[1]
    r_decay, r_aaa, r_gate, r_mv = w2.shape[1], a2.shape[1], g2.shape[1], v1.shape[2]
    ne = router_w.shape[2]
    assert r_decay + r_aaa + r_mv == LORA_W and r_gate == LORA_W
    assert (2 * ca) % cb == 0 and cb == d and ne <= LANE
    t = bsz * seq
    c0 = 2 * ca
    c_lora = c0 + 3 * cb
    c_gate = c_lora + r_decay + r_aaa + r_gate
    col_gate = c0 + 3 * cb
    col_lora = col_gate + 2 * d
    bm = 256

    mod = _modulation(c, w_mod, b_mod)
    x2 = x.reshape(t, d)
    v_first = None
    for l in range(depth):
        mod3 = mod[l].reshape(bsz, 1, 6 * d)
        wl = w_in[l]
        if l == 0:
            v1_cols = jnp.zeros((d, r_mv), F32)
            mu_mv = jnp.zeros((r_mv,), F32)
        else:
            v1_cols = v1[l - 1]
            mu_mv = mu_v[l - 1]
        w_all = jnp.concatenate(
            [wl[:, :c_lora], wl[:, c_gate:], wl[:, c_lora:c_lora + r_decay + r_aaa], v1_cols,
             wl[:, c_lora + r_decay + r_aaa:c_gate]], axis=1).astype(BF)
        mu_l = mu_shift[l]
        mu_lora = jnp.concatenate([mu_l[3 * cb:3 * cb + r_decay + r_aaa], mu_mv, mu_l[3 * cb + r_decay + r_aaa:]])
        zpad = lambda top, w, bot: jnp.concatenate(
            [jnp.zeros((top, cb), F32), w, jnp.zeros((bot, cb), F32)], axis=0).astype(BF)
        w2p = zpad(0, w2[l], LORA_W - r_decay)
        a2p = zpad(r_decay, a2[l], r_mv)

        proj = _inproj(x2, mod3, norm1_g[l], w_all, seq)
        a_conv = _conv_branch(proj, conv_w[l], conv_b[l], conv_ln_g[l], conv_ln_b[l], seq)
        vres = None if l == 0 else (zpad(r_decay + r_aaa, v2[l - 1], 0), v0[l - 1], v_first)
        r, ld, k, v, kk, a, g = _rwkv_prep(proj, mu_l[:3 * cb], mu_lora, w2p, a2p, g2[l].astype(BF), w0[l], a0[l],
                                           k_k[l], k_a[l], vres, seq, cb, c0, col_lora, r_decay)
        if l == 0:
            v_first = v
        yg = _wkv_scan(r, ld, k, v, kk, a, g, r_k[l], gn_g[l], gn_b[l], bsz, seq)
        mixed = _mix_branches(a_conv, yg, proj, w_conv_proj[l].astype(BF), w_rwkv_proj[l].astype(BF), col_gate)

        rw = jnp.pad(router_w[l], ((0, 0), (0, LANE - ne)))
        rw_hi = rw.astype(BF)
        rw_lo = (rw - rw_hi.astype(F32)).astype(BF)
        rb = jnp.pad(router_b[l], (0, LANE - ne)).reshape(1, LANE)
        x2, h, top_i, top_w = _out_and_route(mixed, x2, mod3, norm2_g[l], w_out[l].astype(BF), rw_hi, rw_lo, rb,
                                             seq, ne)
        tables = _routing_tables(top_i[:, :TOP_K], ne, bm)
        y_rows = _experts(h, *tables, w_gate_up[l].astype(BF), b_gate_up[l], w_down[l].astype(BF), b_down[l], bm)
        x2 = _combine(x2, y_rows, top_w, mod3, final_g, seq, final=(l == depth - 1))
    return x2.reshape(bsz, seq, d)
```

```python
import functools

import jax
import jax.numpy as jnp
from jax import lax
from jax.experimental import pallas as pl
from jax.experimental.pallas import tpu as pltpu

F32 = jnp.float32
BF = jnp.bfloat16

HEAD = 64
CHUNK = 64
SCAN_LANES = 4 * HEAD
SCAN_GROUPS = 4
TOP_K = 4
RMS_EPS = 1e-5
LN_EPS = 1e-5
GN_EPS = 64e-5
SWIGLU_LIMIT = 7.0
SWIGLU_ALPHA = 1.702
LORA_W = 256
LANE = 128
SUBLANE = 8
MXU_ROWS = 256
CONV_HALO = 32
SHIFT_HALO = 16
VMEM_LIMIT = 56 * 1024 * 1024


def _params(*sem):
    return pltpu.CompilerParams(dimension_semantics=sem, vmem_limit_bytes=VMEM_LIMIT)


def _tile(n, pref):
    t = min(n, pref)
    while n % t:
        t -= 8
    return t


def _split2(x):
    hi = x.astype(BF)
    lo = (x - hi.astype(F32)).astype(BF)
    return hi, lo


def _split3(x):
    hi = x.astype(BF)
    r1 = x - hi.astype(F32)
    mid = r1.astype(BF)
    lo = (r1 - mid.astype(F32)).astype(BF)
    return hi, mid, lo


def _dot(a, b):
    return jnp.dot(a, b, preferred_element_type=F32)


def _dot_nt(a, b):
    return lax.dot_general(a, b, (((1,), (1,)), ((), ())), preferred_element_type=F32)


def _sigmoid(x):
    return 1.0 / (1.0 + jnp.exp(-x))


def _mod_kernel(c_ref, w_ref, b_ref, o_ref):
    c = c_ref[...]
    cond = (c * _sigmoid(c)).astype(BF)
    o_ref[...] = _dot(cond, w_ref[...].astype(BF)) + b_ref[...]


def _modulation(c, w_mod, b_mod):
    nl, d, n6 = w_mod.shape
    b = c.shape[0]
    tn = _tile(n6, 1024)
    return pl.pallas_call(
        _mod_kernel,
        grid=(nl, n6 // tn),
        in_specs=[pl.BlockSpec((b, d), lambda l, j: (0, 0)),
                  pl.BlockSpec((None, d, tn), lambda l, j: (l, 0, j)),
                  pl.BlockSpec((None, 1, tn), lambda l, j: (l, 0, j))],
        out_specs=pl.BlockSpec((None, b, tn), lambda l, j: (l, 0, j)),
        out_shape=jax.ShapeDtypeStruct((nl, b, n6), F32),
        compiler_params=_params("parallel", "parallel"),
        name="modulation",
    )(c, w_mod, b_mod.reshape(nl, 1, n6))


def _inproj_kernel(x_ref, sh_ref, sc_ref, g_ref, w_ref, o_ref, u_ref):
    @pl.when(pl.program_id(1) == 0)
    def _():
        x = x_ref[...]
        ms = jnp.mean(x * x, axis=-1, keepdims=True)
        xn = x * lax.rsqrt(ms + RMS_EPS) * g_ref[...]
        u_ref[...] = (xn * (1.0 + sc_ref[...]) + sh_ref[...]).astype(BF)

    tm = u_ref.shape[0]
    rb = min(tm, MXU_ROWS)
    for r0 in range(0, tm, rb):
        o_ref[r0:r0 + rb, :] = _dot(u_ref[r0:r0 + rb, :], w_ref[...]).astype(o_ref.dtype)


def _inproj(x2, mod3, g, w, seq):
    t, d = x2.shape
    n = w.shape[1]
    tm = _tile(seq, 1024)
    tn = _tile(n, 1280)
    bidx = lambda i: (i * tm) // seq
    return pl.pallas_call(
        _inproj_kernel,
        grid=(t // tm, n // tn),
        in_specs=[pl.BlockSpec((tm, d), lambda i, j: (i, 0)),
                  pl.BlockSpec((None, 1, d), lambda i, j: (bidx(i), 0, 0)),
                  pl.BlockSpec((None, 1, d), lambda i, j: (bidx(i), 0, 1)),
                  pl.BlockSpec((1, d), lambda i, j: (0, 0)),
                  pl.BlockSpec((d, tn), lambda i, j: (0, j))],
        out_specs=pl.BlockSpec((tm, tn), lambda i, j: (i, j)),
        out_shape=jax.ShapeDtypeStruct((t, n), BF),
        scratch_shapes=[pltpu.VMEM((tm, d), BF)],
        compiler_params=_params("parallel", "arbitrary"),
        name="inproj",
    )(x2, mod3, mod3, g.reshape(1, d), w)


def _conv_kernel(p_ref, h_ref, w_ref, cb_ref, g_ref, b_ref, o_ref, a_ref, sh_ref, y_ref, *, ts, ca, taps, seq):
    i = pl.program_id(0)
    p = p_ref[...].astype(F32)
    a_ref[pl.ds(CONV_HALO, ts), :] = p[:, :ca] * _sigmoid(p[:, ca:])
    hp = h_ref[...].astype(F32)
    keep = jnp.where((i * ts) % seq == 0, 0.0, 1.0)
    a_ref[pl.ds(0, CONV_HALO), :] = hp[:, :ca] * _sigmoid(hp[:, ca:]) * keep

    rb = min(ts, 64)
    off = CONV_HALO - (taps - 1)
    nrow = ts + CONV_HALO - SUBLANE
    for s in range(1, SUBLANE):
        sh_ref[s - 1, 0:nrow, :] = a_ref[s:s + nrow, :]

    for r0 in range(0, ts, rb):
        for c0 in range(0, ca, LANE):
            acc = jnp.zeros((rb, LANE), F32)
            for j in range(taps):
                s = (off + j) % SUBLANE
                q = r0 + off + j - s
                if s == 0:
                    win = a_ref[q:q + rb, c0:c0 + LANE]
                else:
                    win = sh_ref[s - 1, q:q + rb, c0:c0 + LANE]
                acc = acc + win * w_ref[j:j + 1, c0:c0 + LANE]
            y_ref[r0:r0 + rb, c0:c0 + LANE] = acc
    y = y_ref[...] + cb_ref[...]
    mu = jnp.mean(y, axis=-1, keepdims=True)
    yc = y - mu
    var = jnp.mean(yc * yc, axis=-1, keepdims=True)
    yn = yc * lax.rsqrt(var + LN_EPS) * g_ref[...] + b_ref[...]
    o_ref[...] = (yn * _sigmoid(yn)).astype(o_ref.dtype)


def _conv_branch(proj, conv_w, conv_b, ln_g, ln_b, seq):
    t = proj.shape[0]
    taps, _, ca = conv_w.shape
    assert taps - 1 <= CONV_HALO
    ts = _tile(seq, 256)
    hb = ts // CONV_HALO
    kern = functools.partial(_conv_kernel, ts=ts, ca=ca, taps=taps, seq=seq)
    return pl.pallas_call(
        kern,
        grid=(t // ts,),
        in_specs=[pl.BlockSpec((ts, 2 * ca), lambda i: (i, 0)),
                  pl.BlockSpec((CONV_HALO, 2 * ca), lambda i: (jnp.maximum(i * hb - 1, 0), 0)),
                  pl.BlockSpec((taps, ca), lambda i: (0, 0)),
                  pl.BlockSpec((1, ca), lambda i: (0, 0)),
                  pl.BlockSpec((1, ca), lambda i: (0, 0)),
                  pl.BlockSpec((1, ca), lambda i: (0, 0))],
        out_specs=pl.BlockSpec((ts, ca), lambda i: (i, 0)),
        out_shape=jax.ShapeDtypeStruct((t, ca), BF),
        scratch_shapes=[pltpu.VMEM((ts + CONV_HALO, ca), F32),
                        pltpu.VMEM((SUBLANE - 1, ts + CONV_HALO - SUBLANE, ca), F32),
                        pltpu.VMEM((ts, ca), F32)],
        compiler_params=_params("parallel"),
        name="conv_branch",
    )(proj, proj, conv_w.reshape(taps, ca), conv_b.reshape(1, ca), ln_g.reshape(1, ca), ln_b.reshape(1, ca))


def _token_shift(cur_ref, halo_ref, mu, first):
    cur = cur_ref[...].astype(F32)
    last = halo_ref[SHIFT_HALO - 1:SHIFT_HALO, :].astype(F32)
    last = jnp.where(first, 0.0, last)
    row = lax.broadcasted_iota(jnp.int32, cur.shape, 0)
    prev = jnp.where(row == 0, last, pltpu.roll(cur, 1, axis=0))
    return cur + (prev - cur) * mu


def _prep_kernel(*refs, ts, seq, cb, r_decay, has_vres):
    (r_ref, k_ref, v_ref, l_ref, rh_ref, kh_ref, vh_ref, lh_ref, mur_ref, muk_ref, muv_ref, mul_ref,
     w2_ref, a2_ref, g2_ref, w0_ref, a0_ref, kk_ref, ka_ref, e_ref, et_ref) = refs[:21]
    rest = refs[21:]
    if has_vres:
        v2_ref, v0_ref, vf_ref = rest[:3]
        rest = rest[3:]
    ro_ref, ldo_ref, ko_ref, vo_ref, kko_ref, ao_ref, go_ref = rest

    first = (pl.program_id(0) * ts) % seq == 0
    r = _token_shift(r_ref, rh_ref, mur_ref[...], first)
    k = _token_shift(k_ref, kh_ref, muk_ref[...], first)
    v = _token_shift(v_ref, vh_ref, muv_ref[...], first)
    lz = _token_shift(l_ref, lh_ref, mul_ref[...], first)

    l1 = lz[:, :LORA_W]
    lane = lax.broadcasted_iota(jnp.int32, l1.shape, 1)
    t1 = jnp.where(lane < r_decay, jnp.tanh(l1), l1).astype(BF)
    wpre = w0_ref[...] + _dot(t1, w2_ref[...])
    w = -(jnp.maximum(-wpre, 0.0) + jnp.log(1.0 + jnp.exp(-jnp.abs(wpre)))) - 0.5
    ldo_ref[...] = -jnp.exp(w)
    a = _sigmoid(a0_ref[...] + _dot(t1, a2_ref[...]))
    go_ref[...] = _dot(_sigmoid(lz[:, LORA_W:]).astype(BF), g2_ref[...]).astype(go_ref.dtype)
    if has_vres:
        v = v + (vf_ref[...].astype(F32) - v) * _sigmoid(v0_ref[...] + _dot(t1, v2_ref[...]))

    kk = k * kk_ref[...]
    sq_hi, sq_lo = _split2(kk * kk)
    ss = _dot(sq_hi, e_ref[...]) + _dot(sq_lo, e_ref[...])
    inv = lax.rsqrt(jnp.maximum(ss, 1e-24))
    inv_hi, inv_mid, inv_lo = _split3(inv)
    invb = _dot(inv_hi, et_ref[...]) + _dot(inv_mid, et_ref[...]) + _dot(inv_lo, et_ref[...])
    ro_ref[...] = r.astype(ro_ref.dtype)
    ko_ref[...] = (k * (1.0 + (a - 1.0) * ka_ref[...])).astype(ko_ref.dtype)
    vo_ref[...] = v.astype(vo_ref.dtype)
    kko_ref[...] = (kk * invb).astype(kko_ref.dtype)
    ao_ref[...] = a.astype(ao_ref.dtype)


def _rwkv_prep(proj, mu_rkv, mu_lora, w2p, a2p, g2, w0, a0, k_k, k_a, vres, seq, cb, col_r, col_lora, r_decay):
    t = proj.shape[0]
    ts = _tile(seq, 256)
    hb = ts // SHIFT_HALO
    nh = cb // HEAD
    hp = max(nh, LANE)
    head_of = jnp.arange(cb, dtype=jnp.int32) // HEAD
    e = (head_of[:, None] == jnp.arange(hp, dtype=jnp.int32)[None, :]).astype(BF)
    et = e.T
    lw = 2 * LORA_W
    cr = col_r // cb
    cl = col_lora // lw
    halo = lambda i: jnp.maximum(i * hb - 1, 0)
    row = lambda n: pl.BlockSpec((1, n), lambda i: (0, 0))
    full = lambda a, b: pl.BlockSpec((a, b), lambda i: (0, 0))
    in_specs = [pl.BlockSpec((ts, cb), lambda i: (i, cr)),
                pl.BlockSpec((ts, cb), lambda i: (i, cr + 1)),
                pl.BlockSpec((ts, cb), lambda i: (i, cr + 2)),
                pl.BlockSpec((ts, lw), lambda i: (i, cl)),
                pl.BlockSpec((SHIFT_HALO, cb), lambda i: (halo(i), cr)),
                pl.BlockSpec((SHIFT_HALO, cb), lambda i: (halo(i), cr + 1)),
                pl.BlockSpec((SHIFT_HALO, cb), lambda i: (halo(i), cr + 2)),
                pl.BlockSpec((SHIFT_HALO, lw), lambda i: (halo(i), cl)),
                row(cb), row(cb), row(cb), row(lw),
                full(LORA_W, cb), full(LORA_W, cb), full(LORA_W, cb),
                row(cb), row(cb), row(cb), row(cb), full(cb, hp), full(hp, cb)]
    args = [proj, proj, proj, proj, proj, proj, proj, proj,
            mu_rkv[0:cb].reshape(1, cb), mu_rkv[cb:2 * cb].reshape(1, cb), mu_rkv[2 * cb:3 * cb].reshape(1, cb),
            mu_lora.reshape(1, lw), w2p, a2p, g2, w0.reshape(1, cb), a0.reshape(1, cb),
            k_k.reshape(1, cb), k_a.reshape(1, cb), e, et]
    if vres is not None:
        v2p, v0, v_first = vres
        in_specs += [full(LORA_W, cb), row(cb), pl.BlockSpec((ts, cb), lambda i: (i, 0))]
        args += [v2p, v0.reshape(1, cb), v_first]
    tile = pl.BlockSpec((ts, cb), lambda i: (i, 0))
    kern = functools.partial(_prep_kernel, ts=ts, seq=seq, cb=cb, r_decay=r_decay, has_vres=vres is not None)
    out = lambda dt: jax.ShapeDtypeStruct((t, cb), dt)
    return pl.pallas_call(
        kern,
        grid=(t // ts,),
        in_specs=in_specs,
        out_specs=[tile] * 7,
        out_shape=[out(BF), out(F32), out(BF), out(BF), out(BF), out(BF), out(BF)],
        compiler_params=_params("parallel"),
        name="rwkv_prep",
    )(*args)


def _scan_kernel(r_ref, ld_ref, k_ref, v_ref, kk_ref, a_ref, g_ref, rk_ref, gg_ref, gb_ref, o_ref, s_ref,
                 *, nchunk, ng):
    c = CHUNK
    lg = SCAN_LANES
    nq = lg // HEAD
    groups = range(ng)

    @pl.when(pl.program_id(2) == 0)
    def _():
        s_ref[...] = jnp.zeros_like(s_ref)

    row = lax.broadcasted_iota(jnp.int32, (lg, lg), 0)
    col = lax.broadcasted_iota(jnp.int32, (lg, lg), 1)
    same = (row // c) == (col // HEAD)
    bd = same.astype(F32)
    strict = same & (row > col)
    incl = same & (row >= col)
    eye = (row == col).astype(F32)
    avg = (bd * (1.0 / HEAD)).astype(BF)
    ones = bd.astype(BF)
    tr = lax.broadcasted_iota(jnp.int32, (c, c), 0)
    tc = lax.broadcasted_iota(jnp.int32, (c, c), 1)
    ltri = (tr >= tc).astype(BF)

    def stack(x):
        return jnp.concatenate([x] * nq, axis=0) * bd

    def unstack(y):
        out = y[0:c]
        for qi in range(1, nq):
            out = out + y[qi * c:(qi + 1) * c]
        return out

    def body(ci, carry):
        sl = pl.ds(pl.multiple_of(ci * c, c), c)
        ld = ld_ref[sl, :]
        l_hi, l_mid, l_lo = _split3(ld)
        cum = _dot(ltri, l_hi) + _dot(ltri, l_mid) + _dot(ltri, l_lo)
        dinc = jnp.exp(cum)
        dinv = jnp.exp(-cum)
        dexc = jnp.exp(cum - ld)
        kk = kk_ref[sl, :].astype(F32)
        r = r_ref[sl, :].astype(F32)
        k = k_ref[sl, :].astype(F32)
        v = v_ref[sl, :].astype(F32)
        at = -kk * dexc
        rt = r * dinc
        bt = kk * a_ref[sl, :].astype(F32) * dinv
        kt = k * dinv
        dtot = dinc[c - 1:c, :]
        lanes = [slice(gi * lg, (gi + 1) * lg) for gi in groups]

        r_s = [stack(rt[:, ln]) for ln in lanes]
        b_s = [stack(bt[:, ln]) for ln in lanes]
        k_s = [stack(kt[:, ln]) for ln in lanes]
        a_b = [stack(at[:, ln]).astype(BF) for ln in lanes]
        v_b = [stack(v[:, ln]).astype(BF) for ln in lanes]
        r_b = [x.astype(BF) for x in r_s]
        b_b = [x.astype(BF) for x in b_s]
        k_b = [x.astype(BF) for x in k_s]

        a_ab = [jnp.where(strict, _dot_nt(a_b[gi], b_b[gi]), 0.0) for gi in groups]
        a_ak = [jnp.where(strict, _dot_nt(a_b[gi], k_b[gi]), 0.0).astype(BF) for gi in groups]
        a_rb = [jnp.where(incl, _dot_nt(r_b[gi], b_b[gi]), 0.0).astype(BF) for gi in groups]
        a_rk = [jnp.where(incl, _dot_nt(r_b[gi], k_b[gi]), 0.0).astype(BF) for gi in groups]

        tm = [eye + x for x in a_ab]
        pw = a_ab
        for _ in range(c.bit_length() - 2):
            pb = [x.astype(BF) for x in pw]
            pw = [_dot(x, x) for x in pb]
            tm = [tm[gi] + _dot(tm[gi].astype(BF), pw[gi].astype(BF)) for gi in groups]
        tmb = [x.astype(BF) for x in tm]

        akv = [_dot(a_ak[gi], v_b[gi]).astype(BF) for gi in groups]
        wb = [_dot(tmb[gi], a_b[gi]).astype(BF) for gi in groups]
        u0b = [_dot(tmb[gi], akv[gi]).astype(BF) for gi in groups]
        q = [unstack(r_s[gi] + _dot(a_rb[gi], wb[gi])).astype(BF) for gi in groups]
        y0 = [unstack(_dot(a_rb[gi], u0b[gi]) + _dot(a_rk[gi], v_b[gi])) for gi in groups]
        bd_t = [(b_s[gi] * dtot[:, lanes[gi]]).T.astype(BF) for gi in groups]
        kd_t = [(k_s[gi] * dtot[:, lanes[gi]]).T.astype(BF) for gi in groups]
        m = [(eye * dtot[:, lanes[gi]] + _dot(bd_t[gi], wb[gi])).astype(BF) for gi in groups]
        pm = [_dot(bd_t[gi], u0b[gi]) + _dot(kd_t[gi], v_b[gi]) for gi in groups]

        s0 = [s_ref[gi].astype(BF) for gi in groups]
        yt = [_dot(q[gi], s0[gi]) + y0[gi] for gi in groups]
        for gi in groups:
            s_ref[gi] = _dot(m[gi], s0[gi]) + pm[gi]

        for gi in groups:
            ln = lanes[gi]
            y_hi, y_lo = _split2(yt[gi])
            mu = _dot(y_hi, avg) + _dot(y_lo, avg)
            yc = yt[gi] - mu
            s_hi, s_lo = _split2(yc * yc)
            var = _dot(s_hi, avg) + _dot(s_lo, avg)
            yn = yc * lax.rsqrt(var + GN_EPS) * gg_ref[:, ln] + gb_ref[:, ln]
            b_hi, b_lo = _split2(r[:, ln] * k[:, ln] * rk_ref[:, ln])
            bonus = _dot(b_hi, ones) + _dot(b_lo, ones)
            o_ref[sl, ln] = ((yn + bonus * v[:, ln]) * g_ref[sl, ln].astype(F32)).astype(o_ref.dtype)
        return carry

    lax.fori_loop(0, nchunk, body, 0)


def _wkv_scan(r, ld, k, v, kk, a, g, r_k, gn_g, gn_b, bsz, seq):
    t, cb = r.shape
    assert cb % SCAN_LANES == 0
    ng = min(cb // SCAN_LANES, SCAN_GROUPS)
    lb = ng * SCAN_LANES
    sb = _tile(seq, 256)
    nsb = seq // sb
    tile = pl.BlockSpec((sb, lb), lambda b, h, s: (b * nsb + s, h))
    vec = pl.BlockSpec((1, lb), lambda b, h, s: (0, h))
    kern = functools.partial(_scan_kernel, nchunk=sb // CHUNK, ng=ng)
    return pl.pallas_call(
        kern,
        grid=(bsz, cb // lb, nsb),
        in_specs=[tile] * 7 + [vec] * 3,
        out_specs=tile,
        out_shape=jax.ShapeDtypeStruct((t, cb), BF),
        scratch_shapes=[pltpu.VMEM((ng, SCAN_LANES, SCAN_LANES), F32)],
        compiler_params=_params("parallel", "parallel", "arbitrary"),
        name="wkv_scan",
    )(r, ld, k, v, kk, a, g, r_k.reshape(1, cb), gn_g.reshape(1, cb), gn_b.reshape(1, cb))


def _mixa_kernel(a_ref, y_ref, ga_ref, gb_ref, wc_ref, wr_ref, o_ref):
    ya = _dot(a_ref[...], wc_ref[...])
    yb = _dot(y_ref[...], wr_ref[...])
    m = _sigmoid(ga_ref[...].astype(F32)) * ya + _sigmoid(gb_ref[...].astype(F32)) * yb
    o_ref[...] = m.astype(o_ref.dtype)


def _mix_branches(a_conv, yg, proj, wc, wr, col_gate):
    t, ca = a_conv.shape
    cb = yg.shape[1]
    d = wc.shape[1]
    tm = _tile(t, 512)
    cg = col_gate // d
    return pl.pallas_call(
        _mixa_kernel,
        grid=(t // tm,),
        in_specs=[pl.BlockSpec((tm, ca), lambda i: (i, 0)),
                  pl.BlockSpec((tm, cb), lambda i: (i, 0)),
                  pl.BlockSpec((tm, d), lambda i: (i, cg)),
                  pl.BlockSpec((tm, d), lambda i: (i, cg + 1)),
                  pl.BlockSpec((ca, d), lambda i: (0, 0)),
                  pl.BlockSpec((cb, d), lambda i: (0, 0))],
        out_specs=pl.BlockSpec((tm, d), lambda i: (i, 0)),
        out_shape=jax.ShapeDtypeStruct((t, d), BF),
        compiler_params=_params("parallel"),
        name="mix_branches",
    )(a_conv, yg, proj, proj, wc, wr)


def _mixb_kernel(m_ref, x_ref, gt_ref, sh_ref, sc_ref, g_ref, wo_ref, rwh_ref, rwl_ref, rb_ref,
                 xo_ref, ho_ref, ti_ref, tw_ref, *, ne):
    x = x_ref[...] + gt_ref[...] * _dot(m_ref[...], wo_ref[...])
    xo_ref[...] = x
    ms = jnp.mean(x * x, axis=-1, keepdims=True)
    h = x * lax.rsqrt(ms + RMS_EPS) * g_ref[...] * (1.0 + sc_ref[...]) + sh_ref[...]
    ho_ref[...] = h
    h_hi, h_lo = _split2(h)
    logits = _dot(h_hi, rwh_ref[...]) + _dot(h_lo, rwh_ref[...]) + _dot(h_hi, rwl_ref[...]) + rb_ref[...]
    lane = lax.broadcasted_iota(jnp.int32, logits.shape, 1)
    neg = jnp.float32(-jnp.inf)
    cur = jnp.where(lane < ne, logits, neg)
    vals, idxs = [], []
    for _ in range(TOP_K):
        mx = jnp.max(cur, axis=-1, keepdims=True)
        ix = jnp.min(jnp.where(cur == mx, lane, LANE), axis=-1, keepdims=True)
        vals.append(mx)
        idxs.append(ix)
        cur = jnp.where(lane == ix, neg, cur)
    es = [jnp.exp(vk - vals[0]) for vk in vals]
    den = es[0]
    for ek in es[1:]:
        den = den + ek
    ti = jnp.zeros(logits.shape, jnp.int32)
    tw = jnp.zeros(logits.shape, F32)
    for kq in range(TOP_K):
        ti = jnp.where(lane == kq, idxs[kq], ti)
        tw = jnp.where(lane == kq, es[kq] / den, tw)
    ti_ref[...] = ti
    tw_ref[...] = tw


def _out_and_route(m, x2, mod3, g2, wo, rw_hi, rw_lo, rb, seq, ne):
    t, d = x2.shape
    tm = _tile(seq, 256)
    bidx = lambda i: (i * tm) // seq
    modv = lambda which: pl.BlockSpec((None, 1, d), lambda i: (bidx(i), 0, which))
    kern = functools.partial(_mixb_kernel, ne=ne)
    return pl.pallas_call(
        kern,
        grid=(t // tm,),
        in_specs=[pl.BlockSpec((tm, d), lambda i: (i, 0)),
                  pl.BlockSpec((tm, d), lambda i: (i, 0)),
                  modv(2), modv(3), modv(4),
                  pl.BlockSpec((1, d), lambda i: (0, 0)),
                  pl.BlockSpec((d, d), lambda i: (0, 0)),
                  pl.BlockSpec((d, LANE), lambda i: (0, 0)),
                  pl.BlockSpec((d, LANE), lambda i: (0, 0)),
                  pl.BlockSpec((1, LANE), lambda i: (0, 0))],
        out_specs=[pl.BlockSpec((tm, d), lambda i: (i, 0)),
                   pl.BlockSpec((tm, d), lambda i: (i, 0)),
                   pl.BlockSpec((tm, LANE), lambda i: (i, 0)),
                   pl.BlockSpec((tm, LANE), lambda i: (i, 0))],
        out_shape=[jax.ShapeDtypeStruct((t, d), F32), jax.ShapeDtypeStruct((t, d), F32),
                   jax.ShapeDtypeStruct((t, LANE), jnp.int32), jax.ShapeDtypeStruct((t, LANE), F32)],
        compiler_params=_params("parallel"),
        name="out_and_route",
    )(m, x2, mod3, mod3, mod3, g2.reshape(1, d), wo, rw_hi, rw_lo, rb)


def _expert_kernel(be_ref, src_cur, src_nxt, dst_prv, dst_cur, h_hbm, wgu_ref, bgu_ref, wd_ref, bd_ref, o_hbm,
                   x0, x1, y0, y1, gsem, ssem, *, bm, de, nb):
    i = pl.program_id(0)

    def start_gather(src_ref, xb, sem):
        for rw in range(bm):
            pltpu.make_async_copy(h_hbm.at[pl.ds(src_ref[0, rw], 1)], xb.at[pl.ds(rw, 1)], sem).start()

    def wait_gather(xb, sem):
        pltpu.make_async_copy(h_hbm.at[pl.ds(0, bm)], xb, sem).wait()

    def start_scatter(dst_ref, yb, sem):
        for rw in range(bm):
            pltpu.make_async_copy(yb.at[pl.ds(rw, 1)], o_hbm.at[pl.ds(dst_ref[0, rw], 1)], sem).start()

    def wait_scatter(yb, sem):
        pltpu.make_async_copy(yb, o_hbm.at[pl.ds(0, bm)], sem).wait()

    def step(xc, yc, xo, yo, gc, go, sc, so):
        @pl.when(i == 0)
        def _():
            start_gather(src_cur, xc, gc)
            yo[...] = jnp.zeros_like(yo)

        wait_gather(xc, gc)

        @pl.when(be_ref[i] >= 0)
        def _():
            start_gather(src_nxt, xo, go)
            start_scatter(dst_prv, yo, so)

        x = xc[...].astype(BF)
        gu = _dot(x, wgu_ref[...]) + bgu_ref[...]
        gate = jnp.minimum(gu[:, :de], SWIGLU_LIMIT)
        up = jnp.clip(gu[:, de:], -SWIGLU_LIMIT, SWIGLU_LIMIT)
        act = ((up + 1.0) * gate * _sigmoid(SWIGLU_ALPHA * gate)).astype(BF)
        yc[...] = _dot(act, wd_ref[...]) + bd_ref[...]
        wait_scatter(yo, so)

        @pl.when(i == nb - 1)
        def _():
            wait_gather(xo, go)
            start_scatter(dst_cur, yc, sc)
            wait_scatter(yc, sc)

    @pl.when(i % 2 == 0)
    def _():
        step(x0, y0, x1, y1, gsem.at[0], gsem.at[1], ssem.at[0], ssem.at[1])

    @pl.when(i % 2 == 1)
    def _():
        step(x1, y1, x0, y0, gsem.at[1], gsem.at[0], ssem.at[1], ssem.at[0])


def _experts(h, block_expert, src_ext, dst_ext, wgu, bgu, wd, bd, bm):
    t, d = h.shape
    ne, _, de2 = wgu.shape
    de = de2 // 2
    nb = block_expert.shape[0]
    kern = functools.partial(_expert_kernel, bm=bm, de=de, nb=nb)
    smem_row = lambda off: pl.BlockSpec((None, 1, bm), lambda i, be: (i + off, 0, 0), memory_space=pltpu.SMEM)
    grid_spec = pltpu.PrefetchScalarGridSpec(
        num_scalar_prefetch=1,
        grid=(nb,),
        in_specs=[smem_row(0), smem_row(1), smem_row(0), smem_row(1),
                  pl.BlockSpec(memory_space=pl.ANY),
                  pl.BlockSpec((None, d, de2), lambda i, be: (be[i], 0, 0)),
                  pl.BlockSpec((None, 1, de2), lambda i, be: (be[i], 0, 0)),
                  pl.BlockSpec((None, de, d), lambda i, be: (be[i], 0, 0)),
                  pl.BlockSpec((None, 1, d), lambda i, be: (be[i], 0, 0))],
        out_specs=pl.BlockSpec(memory_space=pl.ANY),
        scratch_shapes=[pltpu.VMEM((bm, d), F32), pltpu.VMEM((bm, d), F32),
                        pltpu.VMEM((bm, d), F32), pltpu.VMEM((bm, d), F32),
                        pltpu.SemaphoreType.DMA((2,)), pltpu.SemaphoreType.DMA((2,))],
    )
    src3 = src_ext.reshape(nb + 1, 1, bm)
    dst3 = dst_ext.reshape(nb + 1, 1, bm)
    return pl.pallas_call(
        kern,
        grid_spec=grid_spec,
        out_shape=jax.ShapeDtypeStruct(((nb + 1) * bm, d), F32),
        compiler_params=_params("arbitrary"),
        name="experts",
    )(block_expert, src3, src3, dst3, dst3, h, wgu, bgu.reshape(ne, 1, de2), wd, bd.reshape(ne, 1, d))


def _routing_tables(top_idx, ne, bm):
    t = top_idx.shape[0]
    m = t * TOP_K
    flat_e = top_idx.reshape(m)
    order = jnp.argsort(flat_e).astype(jnp.int32)
    counts = jnp.sum((flat_e[:, None] == jnp.arange(ne, dtype=jnp.int32)[None, :]).astype(jnp.int32), axis=0)
    padded = (counts + bm - 1) // bm * bm
    starts = jnp.cumsum(counts) - counts
    pad_ends = jnp.cumsum(padded)
    pad_starts = pad_ends - padded
    nb = -(-m // bm) + ne
    block_start = jnp.arange(nb, dtype=jnp.int32) * bm
    block_expert = jnp.minimum(jnp.searchsorted(pad_ends, block_start, side='right'), ne - 1).astype(jnp.int32)
    rank = (block_start - pad_starts[block_expert])[:, None] + jnp.arange(bm, dtype=jnp.int32)[None, :]
    valid = (rank >= 0) & (rank < counts[block_expert][:, None])
    assign = order[jnp.clip(starts[block_expert][:, None] + rank, 0, m - 1)]
    src_tok = jnp.where(valid, assign // TOP_K, 0).astype(jnp.int32)
    pad_rank = (jnp.cumsum(jnp.logical_not(valid).reshape(-1).astype(jnp.int32)) - 1).reshape(nb, bm)
    dst_row = jnp.where(valid, (assign % TOP_K) * t + assign // TOP_K, m + pad_rank).astype(jnp.int32)
    first = nb * bm + jnp.arange(bm, dtype=jnp.int32).reshape(1, bm)
    src_ext = jnp.concatenate([src_tok, jnp.zeros((1, bm), jnp.int32)], axis=0)
    dst_ext = jnp.concatenate([first, dst_row], axis=0)
    return block_expert, src_ext, dst_ext


def _combine_kernel(x_ref, y0_ref, y1_ref, y2_ref, y3_ref, tw_ref, gt_ref, fg_ref, o_ref, *, final):
    tw = tw_ref[...]
    acc = tw[:, 0:1] * y0_ref[...]
    for kq, y_ref in enumerate((y1_ref, y2_ref, y3_ref), start=1):
        acc = acc + tw[:, kq:kq + 1] * y_ref[...]
    x = x_ref[...] + gt_ref[...] * acc
    if final:
        ms = jnp.mean(x * x, axis=-1, keepdims=True)
        x = x * lax.rsqrt(ms + RMS_EPS) * fg_ref[...]
    o_ref[...] = x


def _combine(x2, y_rows, top_w, mod3, final_g, seq, final):
    t, d = x2.shape
    assert TOP_K == 4
    tm = _tile(seq, 256)
    nt = t // tm
    bidx = lambda i: (i * tm) // seq
    slot = lambda kq: pl.BlockSpec((tm, d), lambda i: (kq * nt + i, 0))
    kern = functools.partial(_combine_kernel, final=final)
    return pl.pallas_call(
        kern,
        grid=(nt,),
        in_specs=[pl.BlockSpec((tm, d), lambda i: (i, 0)),
                  slot(0), slot(1), slot(2), slot(3),
                  pl.BlockSpec((tm, LANE), lambda i: (i, 0)),
                  pl.BlockSpec((None, 1, d), lambda i: (bidx(i), 0, 5)),
                  pl.BlockSpec((1, d), lambda i: (0, 0))],
        out_specs=pl.BlockSpec((tm, d), lambda i: (i, 0)),
        out_shape=jax.ShapeDtypeStruct((t, d), F32),
        compiler_params=_params("parallel"),
        name="combine",
    )(x2, y_rows, y_rows, y_rows, y_rows, top_w, mod3, final_g.reshape(1, d))


def kernel(x, c, norm1_g, norm2_g, w_mod, b_mod, w_in, conv_w, conv_b, conv_ln_g, conv_ln_b, w_conv_proj, mu_shift, w0, w2, a0, a2, g2, k_k, k_a, r_k, gn_g, gn_b, w_rwkv_proj, v0, v1, mu_v, v2, w_out, router_w, router_b, w_gate_up, b_gate_up, w_down, b_down, final_g):
    bsz, seq, d = x.shape
    depth = w_mod.shape[0]
    ca = conv_w.shape[-1]
    cb = w0.shape[1]
    r_decay, r_aaa, r_gate, r_mv = w2.shape[1], a2.shape[1], g2.shape[1], v1.shape[2]
    ne = router_w.shape[2]
    assert r_decay + r_aaa + r_mv == LORA_W and r_gate == LORA_W
    assert (2 * ca) % cb == 0 and cb == d and ne <= LANE
    t = bsz * seq
    c0 = 2 * ca
    c_lora = c0 + 3 * cb
    c_gate = c_lora + r_decay + r_aaa + r_gate
    col_gate = c0 + 3 * cb
    col_lora = col_gate + 2 * d
    bm = 256

    mod = _modulation(c, w_mod, b_mod)
    x2 = x.reshape(t, d)
    v_first = None
    for l in range(depth):
        mod3 = mod[l].reshape(bsz, 1, 6 * d)
        wl = w_in[l]
        if l == 0:
            v1_cols = jnp.zeros((d, r_mv), F32)
            mu_mv = jnp.zeros((r_mv,), F32)
        else:
            v1_cols = v1[l - 1]
            mu_mv = mu_v[l - 1]
        w_all = jnp.concatenate(
            [wl[:, :c_lora], wl[:, c_gate:], wl[:, c_lora:c_lora + r_decay + r_aaa], v1_cols,
             wl[:, c_lora + r_decay + r_aaa:c_gate]], axis=1).astype(BF)
        mu_l = mu_shift[l]
        mu_lora = jnp.concatenate([mu_l[3 * cb:3 * cb + r_decay + r_aaa], mu_mv, mu_l[3 * cb + r_decay + r_aaa:]])
        zpad = lambda top, w, bot: jnp.concatenate(
            [jnp.zeros((top, cb), F32), w, jnp.zeros((bot, cb), F32)], axis=0).astype(BF)
        w2p = zpad(0, w2[l], LORA_W - r_decay)
        a2p = zpad(r_decay, a2[l], r_mv)

        proj = _inproj(x2, mod3, norm1_g[l], w_all, seq)
        a_conv = _conv_branch(proj, conv_w[l], conv_b[l], conv_ln_g[l], conv_ln_b[l], seq)
        vres = None if l == 0 else (zpad(r_decay + r_aaa, v2[l - 1], 0), v0[l - 1], v_first)
        r, ld, k, v, kk, a, g = _rwkv_prep(proj, mu_l[:3 * cb], mu_lora, w2p, a2p, g2[l].astype(BF), w0[l], a0[l],
                                           k_k[l], k_a[l], vres, seq, cb, c0, col_lora, r_decay)
        if l == 0:
            v_first = v
        yg = _wkv_scan(r, ld, k, v, kk, a, g, r_k[l], gn_g[l], gn_b[l], bsz, seq)
        mixed = _mix_branches(a_conv, yg, proj, w_conv_proj[l].astype(BF), w_rwkv_proj[l].astype(BF), col_gate)

        rw = jnp.pad(router_w[l], ((0, 0), (0, LANE - ne)))
        rw_hi = rw.astype(BF)
        rw_lo = (rw - rw_hi.astype(F32)).astype(BF)
        rb = jnp.pad(router_b[l], (0, LANE - ne)).reshape(1, LANE)
        x2, h, top_i, top_w = _out_and_route(mixed, x2, mod3, norm2_g[l], w_out[l].astype(BF), rw_hi, rw_lo, rb,
                                             seq, ne)
        tables = _routing_tables(top_i[:, :TOP_K], ne, bm)
        y_rows = _experts(h, *tables, w_gate_up[l].astype(BF), b_gate_up[l], w_down[l].astype(BF), b_down[l], bm)
        x2 = _combine(x2, y_rows, top_w, mod3, final_g, seq, final=(l == depth - 1))
    return x2.reshape(bsz, seq, d)
```

```python
import functools

import jax
import jax.numpy as jnp
from jax import lax
from jax.experimental import pallas as pl
from jax.experimental.pallas import tpu as pltpu

F32 = jnp.float32
BF = jnp.bfloat16

HEAD = 64
CHUNK = 64
SCAN_LANES = 4 * HEAD
SCAN_GROUPS = 8
TOP_K = 4
RMS_EPS = 1e-5
LN_EPS = 1e-5
GN_EPS = 64e-5
DECAY_SCALE = 0.6065306597126334
SWIGLU_LIMIT = 7.0
SWIGLU_ALPHA = 1.702
LORA_W = 256
LANE = 128
SUBLANE = 8
MXU_ROWS = 256
CONV_HALO = 32
SHIFT_HALO = 16
VMEM_LIMIT = 56 * 1024 * 1024


def _params(*sem):
    return pltpu.CompilerParams(dimension_semantics=sem, vmem_limit_bytes=VMEM_LIMIT)


def _tile(n, pref):
    t = min(n, pref)
    while n % t:
        t -= 8
    return t


def _split2(x):
    hi = x.astype(BF)
    lo = (x - hi.astype(F32)).astype(BF)
    return hi, lo


def _split3(x):
    hi = x.astype(BF)
    r1 = x - hi.astype(F32)
    mid = r1.astype(BF)
    lo = (r1 - mid.astype(F32)).astype(BF)
    return hi, mid, lo


def _dot(a, b):
    return jnp.dot(a, b, preferred_element_type=F32)


def _dot_nt(a, b):
    return lax.dot_general(a, b, (((1,), (1,)), ((), ())), preferred_element_type=F32)


def _sigmoid(x):
    return 1.0 / (1.0 + jnp.exp(-x))


def _mod_kernel(c_ref, w_ref, b_ref, o_ref):
    c = c_ref[...]
    cond = (c * _sigmoid(c)).astype(BF)
    o_ref[...] = _dot(cond, w_ref[...].astype(BF)) + b_ref[...]


def _modulation(c, w_mod, b_mod):
    nl, d, n6 = w_mod.shape
    b = c.shape[0]
    tn = _tile(n6, 1024)
    return pl.pallas_call(
        _mod_kernel,
        grid=(nl, n6 // tn),
        in_specs=[pl.BlockSpec((b, d), lambda l, j: (0, 0)),
                  pl.BlockSpec((None, d, tn), lambda l, j: (l, 0, j)),
                  pl.BlockSpec((None, 1, tn), lambda l, j: (l, 0, j))],
        out_specs=pl.BlockSpec((None, b, tn), lambda l, j: (l, 0, j)),
        out_shape=jax.ShapeDtypeStruct((nl, b, n6), F32),
        compiler_params=_params("parallel", "parallel"),
        name="modulation",
    )(c, w_mod, b_mod.reshape(nl, 1, n6))


def _inproj_kernel(x_ref, sh_ref, sc_ref, g_ref, w_ref, o_ref, u_ref):
    @pl.when(pl.program_id(1) == 0)
    def _():
        x = x_ref[...]
        ms = jnp.mean(x * x, axis=-1, keepdims=True)
        xn = x * lax.rsqrt(ms + RMS_EPS) * g_ref[...]
        u_ref[...] = (xn * (1.0 + sc_ref[...]) + sh_ref[...]).astype(BF)

    tm = u_ref.shape[0]
    rb = min(tm, MXU_ROWS)
    for r0 in range(0, tm, rb):
        o_ref[r0:r0 + rb, :] = _dot(u_ref[r0:r0 + rb, :], w_ref[...]).astype(o_ref.dtype)


def _inproj(x2, mod3, g, w, seq):
    t, d = x2.shape
    n = w.shape[1]
    tm = _tile(seq, 1024)
    tn = _tile(n, 1280)
    bidx = lambda i: (i * tm) // seq
    return pl.pallas_call(
        _inproj_kernel,
        grid=(t // tm, n // tn),
        in_specs=[pl.BlockSpec((tm, d), lambda i, j: (i, 0)),
                  pl.BlockSpec((None, 1, d), lambda i, j: (bidx(i), 0, 0)),
                  pl.BlockSpec((None, 1, d), lambda i, j: (bidx(i), 0, 1)),
                  pl.BlockSpec((1, d), lambda i, j: (0, 0)),
                  pl.BlockSpec((d, tn), lambda i, j: (0, j))],
        out_specs=pl.BlockSpec((tm, tn), lambda i, j: (i, j)),
        out_shape=jax.ShapeDtypeStruct((t, n), BF),
        scratch_shapes=[pltpu.VMEM((tm, d), BF)],
        compiler_params=_params("parallel", "arbitrary"),
        name="inproj",
    )(x2, mod3, mod3, g.reshape(1, d), w)


def _conv_kernel(p_ref, h_ref, w_ref, cb_ref, g_ref, b_ref, o_ref, a_ref, sh_ref, y_ref, *, ts, ca, taps, seq):
    i = pl.program_id(0)
    p = p_ref[...].astype(F32)
    a_ref[pl.ds(CONV_HALO, ts), :] = p[:, :ca] * _sigmoid(p[:, ca:])
    hp = h_ref[...].astype(F32)
    keep = jnp.where((i * ts) % seq == 0, 0.0, 1.0)
    a_ref[pl.ds(0, CONV_HALO), :] = hp[:, :ca] * _sigmoid(hp[:, ca:]) * keep

    rb = min(ts, 64)
    off = CONV_HALO - (taps - 1)
    nrow = ts + CONV_HALO - SUBLANE
    for s in range(1, SUBLANE):
        sh_ref[s - 1, 0:nrow, :] = a_ref[s:s + nrow, :]

    for r0 in range(0, ts, rb):
        for c0 in range(0, ca, LANE):
            acc = jnp.zeros((rb, LANE), F32)
            for j in range(taps):
                s = (off + j) % SUBLANE
                q = r0 + off + j - s
                if s == 0:
                    win = a_ref[q:q + rb, c0:c0 + LANE]
                else:
                    win = sh_ref[s - 1, q:q + rb, c0:c0 + LANE]
                acc = acc + win * w_ref[j:j + 1, c0:c0 + LANE]
            y_ref[r0:r0 + rb, c0:c0 + LANE] = acc
    y = y_ref[...] + cb_ref[...]
    mu = jnp.mean(y, axis=-1, keepdims=True)
    yc = y - mu
    var = jnp.mean(yc * yc, axis=-1, keepdims=True)
    yn = yc * lax.rsqrt(var + LN_EPS) * g_ref[...] + b_ref[...]
    o_ref[...] = (yn * _sigmoid(yn)).astype(o_ref.dtype)


def _conv_branch(proj, conv_w, conv_b, ln_g, ln_b, seq):
    t = proj.shape[0]
    taps, _, ca = conv_w.shape
    assert taps - 1 <= CONV_HALO
    ts = _tile(seq, 256)
    hb = ts // CONV_HALO
    kern = functools.partial(_conv_kernel, ts=ts, ca=ca, taps=taps, seq=seq)
    return pl.pallas_call(
        kern,
        grid=(t // ts,),
        in_specs=[pl.BlockSpec((ts, 2 * ca), lambda i: (i, 0)),
                  pl.BlockSpec((CONV_HALO, 2 * ca), lambda i: (jnp.maximum(i * hb - 1, 0), 0)),
                  pl.BlockSpec((taps, ca), lambda i: (0, 0)),
                  pl.BlockSpec((1, ca), lambda i: (0, 0)),
                  pl.BlockSpec((1, ca), lambda i: (0, 0)),
                  pl.BlockSpec((1, ca), lambda i: (0, 0))],
        out_specs=pl.BlockSpec((ts, ca), lambda i: (i, 0)),
        out_shape=jax.ShapeDtypeStruct((t, ca), BF),
        scratch_shapes=[pltpu.VMEM((ts + CONV_HALO, ca), F32),
                        pltpu.VMEM((SUBLANE - 1, ts + CONV_HALO - SUBLANE, ca), F32),
                        pltpu.VMEM((ts, ca), F32)],
        compiler_params=_params("parallel"),
        name="conv_branch",
    )(proj, proj, conv_w.reshape(taps, ca), conv_b.reshape(1, ca), ln_g.reshape(1, ca), ln_b.reshape(1, ca))


def _token_shift(cur_ref, halo_ref, mu, first):
    cur = cur_ref[...].astype(F32)
    last = halo_ref[SHIFT_HALO - 1:SHIFT_HALO, :].astype(F32)
    last = jnp.where(first, 0.0, last)
    row = lax.broadcasted_iota(jnp.int32, cur.shape, 0)
    prev = jnp.where(row == 0, last, pltpu.roll(cur, 1, axis=0))
    return cur + (prev - cur) * mu


def _prep_kernel(*refs, ts, seq, cb, r_decay, has_vres):
    (r_ref, k_ref, v_ref, l_ref, rh_ref, kh_ref, vh_ref, lh_ref, mur_ref, muk_ref, muv_ref, mul_ref,
     w2_ref, a2_ref, g2_ref, w0_ref, a0_ref, kk_ref, ka_ref, e_ref, et_ref) = refs[:21]
    rest = refs[21:]
    if has_vres:
        v2_ref, v0_ref, vf_ref = rest[:3]
        rest = rest[3:]
    ro_ref, ldo_ref, ko_ref, vo_ref, kko_ref, ao_ref, go_ref = rest

    first = (pl.program_id(0) * ts) % seq == 0
    r = _token_shift(r_ref, rh_ref, mur_ref[...], first)
    k = _token_shift(k_ref, kh_ref, muk_ref[...], first)
    v = _token_shift(v_ref, vh_ref, muv_ref[...], first)
    lz = _token_shift(l_ref, lh_ref, mul_ref[...], first)

    l1 = lz[:, :LORA_W]
    lane = lax.broadcasted_iota(jnp.int32, l1.shape, 1)
    t1 = jnp.where(lane < r_decay, jnp.tanh(l1), l1).astype(BF)
    wpre = w0_ref[...] + _dot(t1, w2_ref[...])
    ldo_ref[...] = -DECAY_SCALE * _sigmoid(wpre)
    a = _sigmoid(a0_ref[...] + _dot(t1, a2_ref[...]))
    go_ref[...] = _dot(_sigmoid(lz[:, LORA_W:]).astype(BF), g2_ref[...]).astype(go_ref.dtype)
    if has_vres:
        v = v + (vf_ref[...].astype(F32) - v) * _sigmoid(v0_ref[...] + _dot(t1, v2_ref[...]))

    kk = k * kk_ref[...]
    sq_hi, sq_lo = _split2(kk * kk)
    ss = _dot(sq_hi, e_ref[...]) + _dot(sq_lo, e_ref[...])
    inv = lax.rsqrt(jnp.maximum(ss, 1e-24))
    inv_hi, inv_mid, inv_lo = _split3(inv)
    invb = _dot(inv_hi, et_ref[...]) + _dot(inv_mid, et_ref[...]) + _dot(inv_lo, et_ref[...])
    ro_ref[...] = r.astype(ro_ref.dtype)
    ko_ref[...] = (k * (1.0 + (a - 1.0) * ka_ref[...])).astype(ko_ref.dtype)
    vo_ref[...] = v.astype(vo_ref.dtype)
    kko_ref[...] = (kk * invb).astype(kko_ref.dtype)
    ao_ref[...] = a.astype(ao_ref.dtype)


def _rwkv_prep(proj, mu_rkv, mu_lora, w2p, a2p, g2, w0, a0, k_k, k_a, vres, seq, cb, col_r, col_lora, r_decay):
    t = proj.shape[0]
    ts = _tile(seq, 256)
    hb = ts // SHIFT_HALO
    nh = cb // HEAD
    hp = max(nh, LANE)
    head_of = jnp.arange(cb, dtype=jnp.int32) // HEAD
    e = (head_of[:, None] == jnp.arange(hp, dtype=jnp.int32)[None, :]).astype(BF)
    et = e.T
    lw = 2 * LORA_W
    cr = col_r // cb
    cl = col_lora // lw
    halo = lambda i: jnp.maximum(i * hb - 1, 0)
    row = lambda n: pl.BlockSpec((1, n), lambda i: (0, 0))
    full = lambda a, b: pl.BlockSpec((a, b), lambda i: (0, 0))
    in_specs = [pl.BlockSpec((ts, cb), lambda i: (i, cr)),
                pl.BlockSpec((ts, cb), lambda i: (i, cr + 1)),
                pl.BlockSpec((ts, cb), lambda i: (i, cr + 2)),
                pl.BlockSpec((ts, lw), lambda i: (i, cl)),
                pl.BlockSpec((SHIFT_HALO, cb), lambda i: (halo(i), cr)),
                pl.BlockSpec((SHIFT_HALO, cb), lambda i: (halo(i), cr + 1)),
                pl.BlockSpec((SHIFT_HALO, cb), lambda i: (halo(i), cr + 2)),
                pl.BlockSpec((SHIFT_HALO, lw), lambda i: (halo(i), cl)),
                row(cb), row(cb), row(cb), row(lw),
                full(LORA_W, cb), full(LORA_W, cb), full(LORA_W, cb),
                row(cb), row(cb), row(cb), row(cb), full(cb, hp), full(hp, cb)]
    args = [proj, proj, proj, proj, proj, proj, proj, proj,
            mu_rkv[0:cb].reshape(1, cb), mu_rkv[cb:2 * cb].reshape(1, cb), mu_rkv[2 * cb:3 * cb].reshape(1, cb),
            mu_lora.reshape(1, lw), w2p, a2p, g2, w0.reshape(1, cb), a0.reshape(1, cb),
            k_k.reshape(1, cb), k_a.reshape(1, cb), e, et]
    if vres is not None:
        v2p, v0, v_first = vres
        in_specs += [full(LORA_W, cb), row(cb), pl.BlockSpec((ts, cb), lambda i: (i, 0))]
        args += [v2p, v0.reshape(1, cb), v_first]
    tile = pl.BlockSpec((ts, cb), lambda i: (i, 0))
    kern = functools.partial(_prep_kernel, ts=ts, seq=seq, cb=cb, r_decay=r_decay, has_vres=vres is not None)
    out = lambda dt: jax.ShapeDtypeStruct((t, cb), dt)
    return pl.pallas_call(
        kern,
        grid=(t // ts,),
        in_specs=in_specs,
        out_specs=[tile] * 7,
        out_shape=[out(BF), out(F32), out(BF), out(BF), out(BF), out(BF), out(BF)],
        compiler_params=_params("parallel"),
        name="rwkv_prep",
    )(*args)


def _scan_kernel(r_ref, ld_ref, k_ref, v_ref, kk_ref, a_ref, g_ref, rk_ref, gg_ref, gb_ref, o_ref, s_ref,
                 *, nchunk, ng):
    c = CHUNK
    lg = SCAN_LANES
    nq = lg // HEAD
    groups = range(ng)

    @pl.when(pl.program_id(2) == 0)
    def _():
        s_ref[...] = jnp.zeros_like(s_ref)

    row = lax.broadcasted_iota(jnp.int32, (lg, lg), 0)
    col = lax.broadcasted_iota(jnp.int32, (lg, lg), 1)
    same = (row // c) == (col // HEAD)
    bd = same.astype(F32)
    strict = same & (row > col)
    incl = same & (row >= col)
    eye = (row == col).astype(F32)
    avg = (bd * (1.0 / HEAD)).astype(BF)
    ones = bd.astype(BF)
    tr = lax.broadcasted_iota(jnp.int32, (c, c), 0)
    tc = lax.broadcasted_iota(jnp.int32, (c, c), 1)
    ltri = (tr >= tc).astype(BF)

    def stack(x):
        return jnp.concatenate([x] * nq, axis=0) * bd

    def unstack(y):
        out = y[0:c]
        for qi in range(1, nq):
            out = out + y[qi * c:(qi + 1) * c]
        return out

    def body(ci, carry):
        sl = pl.ds(pl.multiple_of(ci * c, c), c)
        ld = ld_ref[sl, :]
        l_hi, l_mid, l_lo = _split3(ld)
        cum = _dot(ltri, l_hi) + _dot(ltri, l_mid) + _dot(ltri, l_lo)
        dinc = jnp.exp(cum)
        dinv = jnp.exp(-cum)
        dexc = jnp.exp(cum - ld)
        kk = kk_ref[sl, :].astype(F32)
        r = r_ref[sl, :].astype(F32)
        k = k_ref[sl, :].astype(F32)
        v = v_ref[sl, :].astype(F32)
        at = -kk * dexc
        rt = r * dinc
        bt = kk * a_ref[sl, :].astype(F32) * dinv
        kt = k * dinv
        dtot = dinc[c - 1:c, :]
        lanes = [slice(gi * lg, (gi + 1) * lg) for gi in groups]

        r_s = [stack(rt[:, ln]) for ln in lanes]
        b_s = [stack(bt[:, ln]) for ln in lanes]
        k_s = [stack(kt[:, ln]) for ln in lanes]
        a_b = [stack(at[:, ln]).astype(BF) for ln in lanes]
        v_b = [stack(v[:, ln]).astype(BF) for ln in lanes]
        r_b = [x.astype(BF) for x in r_s]
        b_b = [x.astype(BF) for x in b_s]
        k_b = [x.astype(BF) for x in k_s]

        a_ab = [jnp.where(strict, _dot_nt(a_b[gi], b_b[gi]), 0.0) for gi in groups]
        a_ak = [jnp.where(strict, _dot_nt(a_b[gi], k_b[gi]), 0.0).astype(BF) for gi in groups]
        a_rb = [jnp.where(incl, _dot_nt(r_b[gi], b_b[gi]), 0.0).astype(BF) for gi in groups]
        a_rk = [jnp.where(incl, _dot_nt(r_b[gi], k_b[gi]), 0.0).astype(BF) for gi in groups]

        tm = [eye + x for x in a_ab]
        pw = a_ab
        for _ in range(c.bit_length() - 2):
            pb = [x.astype(BF) for x in pw]
            pw = [_dot(x, x) for x in pb]
            tm = [tm[gi] + _dot(tm[gi].astype(BF), pw[gi].astype(BF)) for gi in groups]
        tmb = [x.astype(BF) for x in tm]

        akv = [_dot(a_ak[gi], v_b[gi]).astype(BF) for gi in groups]
        wb = [_dot(tmb[gi], a_b[gi]).astype(BF) for gi in groups]
        u0b = [_dot(tmb[gi], akv[gi]).astype(BF) for gi in groups]
        q = [unstack(r_s[gi] + _dot(a_rb[gi], wb[gi])).astype(BF) for gi in groups]
        y0 = [unstack(_dot(a_rb[gi], u0b[gi]) + _dot(a_rk[gi], v_b[gi])) for gi in groups]
        bd_t = [(b_s[gi] * dtot[:, lanes[gi]]).T.astype(BF) for gi in groups]
        kd_t = [(k_s[gi] * dtot[:, lanes[gi]]).T.astype(BF) for gi in groups]
        m = [(eye * dtot[:, lanes[gi]] + _dot(bd_t[gi], wb[gi])).astype(BF) for gi in groups]
        pm = [_dot(bd_t[gi], u0b[gi]) + _dot(kd_t[gi], v_b[gi]) for gi in groups]

        s0 = [s_ref[gi].astype(BF) for gi in groups]
        yt = [_dot(q[gi], s0[gi]) + y0[gi] for gi in groups]
        for gi in groups:
            s_ref[gi] = _dot(m[gi], s0[gi]) + pm[gi]

        for gi in groups:
            ln = lanes[gi]
            mu = _dot(yt[gi].astype(BF), avg)
            yc = yt[gi] - mu
            var = _dot((yc * yc).astype(BF), avg)
            yn = yc * lax.rsqrt(var + GN_EPS) * gg_ref[:, ln] + gb_ref[:, ln]
            bonus = _dot((r[:, ln] * k[:, ln] * rk_ref[:, ln]).astype(BF), ones)
            o_ref[sl, ln] = ((yn + bonus * v[:, ln]) * g_ref[sl, ln].astype(F32)).astype(o_ref.dtype)
        return carry

    lax.fori_loop(0, nchunk, body, 0)


def _wkv_scan(r, ld, k, v, kk, a, g, r_k, gn_g, gn_b, bsz, seq):
    t, cb = r.shape
    assert cb % SCAN_LANES == 0
    ng = min(cb // SCAN_LANES, SCAN_GROUPS)
    lb = ng * SCAN_LANES
    sb = _tile(seq, 256)
    nsb = seq // sb
    tile = pl.BlockSpec((sb, lb), lambda b, h, s: (b * nsb + s, h))
    vec = pl.BlockSpec((1, lb), lambda b, h, s: (0, h))
    kern = functools.partial(_scan_kernel, nchunk=sb // CHUNK, ng=ng)
    return pl.pallas_call(
        kern,
        grid=(bsz, cb // lb, nsb),
        in_specs=[tile] * 7 + [vec] * 3,
        out_specs=tile,
        out_shape=jax.ShapeDtypeStruct((t, cb), BF),
        scratch_shapes=[pltpu.VMEM((ng, SCAN_LANES, SCAN_LANES), F32)],
        compiler_params=_params("parallel", "parallel", "arbitrary"),
        name="wkv_scan",
    )(r, ld, k, v, kk, a, g, r_k.reshape(1, cb), gn_g.reshape(1, cb), gn_b.reshape(1, cb))


def _mixa_kernel(a_ref, y_ref, ga_ref, gb_ref, wc_ref, wr_ref, o_ref):
    ya = _dot(a_ref[...], wc_ref[...])
    yb = _dot(y_ref[...], wr_ref[...])
    m = _sigmoid(ga_ref[...].astype(F32)) * ya + _sigmoid(gb_ref[...].astype(F32)) * yb
    o_ref[...] = m.astype(o_ref.dtype)


def _mix_branches(a_conv, yg, proj, wc, wr, col_gate):
    t, ca = a_conv.shape
    cb = yg.shape[1]
    d = wc.shape[1]
    tm = _tile(t, 512)
    cg = col_gate // d
    return pl.pallas_call(
        _mixa_kernel,
        grid=(t // tm,),
        in_specs=[pl.BlockSpec((tm, ca), lambda i: (i, 0)),
                  pl.BlockSpec((tm, cb), lambda i: (i, 0)),
                  pl.BlockSpec((tm, d), lambda i: (i, cg)),
                  pl.BlockSpec((tm, d), lambda i: (i, cg + 1)),
                  pl.BlockSpec((ca, d), lambda i: (0, 0)),
                  pl.BlockSpec((cb, d), lambda i: (0, 0))],
        out_specs=pl.BlockSpec((tm, d), lambda i: (i, 0)),
        out_shape=jax.ShapeDtypeStruct((t, d), BF),
        compiler_params=_params("parallel"),
        name="mix_branches",
    )(a_conv, yg, proj, proj, wc, wr)


def _mixb_kernel(m_ref, x_ref, gt_ref, sh_ref, sc_ref, g_ref, wo_ref, rwh_ref, rwl_ref, rb_ref,
                 xo_ref, ho_ref, ti_ref, tw_ref, *, ne):
    x = x_ref[...] + gt_ref[...] * _dot(m_ref[...], wo_ref[...])
    xo_ref[...] = x
    ms = jnp.mean(x * x, axis=-1, keepdims=True)
    h = x * lax.rsqrt(ms + RMS_EPS) * g_ref[...] * (1.0 + sc_ref[...]) + sh_ref[...]
    ho_ref[...] = h
    h_hi, h_lo = _split2(h)
    logits = _dot(h_hi, rwh_ref[...]) + _dot(h_lo, rwh_ref[...]) + _dot(h_hi, rwl_ref[...]) + rb_ref[...]
    lane = lax.broadcasted_iota(jnp.int32, logits.shape, 1)
    neg = jnp.float32(-jnp.inf)
    cur = jnp.where(lane < ne, logits, neg)
    vals, idxs = [], []
    for _ in range(TOP_K):
        mx = jnp.max(cur, axis=-1, keepdims=True)
        ix = jnp.min(jnp.where(cur == mx, lane, LANE), axis=-1, keepdims=True)
        vals.append(mx)
        idxs.append(ix)
        cur = jnp.where(lane == ix, neg, cur)
    es = [jnp.exp(vk - vals[0]) for vk in vals]
    den = es[0]
    for ek in es[1:]:
        den = den + ek
    ti = jnp.zeros(logits.shape, jnp.int32)
    tw = jnp.zeros(logits.shape, F32)
    for kq in range(TOP_K):
        ti = jnp.where(lane == kq, idxs[kq], ti)
        tw = jnp.where(lane == kq, es[kq] / den, tw)
    ti_ref[...] = ti
    tw_ref[...] = tw


def _out_and_route(m, x2, mod3, g2, wo, rw_hi, rw_lo, rb, seq, ne):
    t, d = x2.shape
    tm = _tile(seq, 256)
    bidx = lambda i: (i * tm) // seq
    modv = lambda which: pl.BlockSpec((None, 1, d), lambda i: (bidx(i), 0, which))
    kern = functools.partial(_mixb_kernel, ne=ne)
    return pl.pallas_call(
        kern,
        grid=(t // tm,),
        in_specs=[pl.BlockSpec((tm, d), lambda i: (i, 0)),
                  pl.BlockSpec((tm, d), lambda i: (i, 0)),
                  modv(2), modv(3), modv(4),
                  pl.BlockSpec((1, d), lambda i: (0, 0)),
                  pl.BlockSpec((d, d), lambda i: (0, 0)),
                  pl.BlockSpec((d, LANE), lambda i: (0, 0)),
                  pl.BlockSpec((d, LANE), lambda i: (0, 0)),
                  pl.BlockSpec((1, LANE), lambda i: (0, 0))],
        out_specs=[pl.BlockSpec((tm, d), lambda i: (i, 0)),
                   pl.BlockSpec((tm, d), lambda i: (i, 0)),
                   pl.BlockSpec((tm, LANE), lambda i: (i, 0)),
                   pl.BlockSpec((tm, LANE), lambda i: (i, 0))],
        out_shape=[jax.ShapeDtypeStruct((t, d), F32), jax.ShapeDtypeStruct((t, d), F32),
                   jax.ShapeDtypeStruct((t, LANE), jnp.int32), jax.ShapeDtypeStruct((t, LANE), F32)],
        compiler_params=_params("parallel"),
        name="out_and_route",
    )(m, x2, mod3, mod3, mod3, g2.reshape(1, d), wo, rw_hi, rw_lo, rb)


def _expert_kernel(be_ref, src_cur, src_nxt, dst_prv, dst_cur, h_hbm, wgu_ref, bgu_ref, wd_ref, bd_ref, o_hbm,
                   x0, x1, y0, y1, gsem, ssem, *, bm, de, nb):
    i = pl.program_id(0)

    def start_gather(src_ref, xb, sem):
        for rw in range(bm):
            pltpu.make_async_copy(h_hbm.at[pl.ds(src_ref[0, rw], 1)], xb.at[pl.ds(rw, 1)], sem).start()

    def wait_gather(xb, sem):
        pltpu.make_async_copy(h_hbm.at[pl.ds(0, bm)], xb, sem).wait()

    def start_scatter(dst_ref, yb, sem):
        for rw in range(bm):
            pltpu.make_async_copy(yb.at[pl.ds(rw, 1)], o_hbm.at[pl.ds(dst_ref[0, rw], 1)], sem).start()

    def wait_scatter(yb, sem):
        pltpu.make_async_copy(yb, o_hbm.at[pl.ds(0, bm)], sem).wait()

    def step(xc, yc, xo, yo, gc, go, sc, so):
        @pl.when(i == 0)
        def _():
            start_gather(src_cur, xc, gc)
            yo[...] = jnp.zeros_like(yo)

        wait_gather(xc, gc)

        @pl.when(be_ref[i] >= 0)
        def _():
            start_gather(src_nxt, xo, go)
            start_scatter(dst_prv, yo, so)

        x = xc[...].astype(BF)
        gu = _dot(x, wgu_ref[...]) + bgu_ref[...]
        gate = jnp.minimum(gu[:, :de], SWIGLU_LIMIT)
        up = jnp.clip(gu[:, de:], -SWIGLU_LIMIT, SWIGLU_LIMIT)
        act = ((up + 1.0) * gate * _sigmoid(SWIGLU_ALPHA * gate)).astype(BF)
        yc[...] = _dot(act, wd_ref[...]) + bd_ref[...]
        wait_scatter(yo, so)

        @pl.when(i == nb - 1)
        def _():
            wait_gather(xo, go)
            start_scatter(dst_cur, yc, sc)
            wait_scatter(yc, sc)

    @pl.when(i % 2 == 0)
    def _():
        step(x0, y0, x1, y1, gsem.at[0], gsem.at[1], ssem.at[0], ssem.at[1])

    @pl.when(i % 2 == 1)
    def _():
        step(x1, y1, x0, y0, gsem.at[1], gsem.at[0], ssem.at[1], ssem.at[0])


def _experts(h, block_expert, src_ext, dst_ext, wgu, bgu, wd, bd, layer, bm):
    t, d = h.shape
    nl, ne, _, de2 = wgu.shape
    de = de2 // 2
    nb = block_expert.shape[0]
    kern = functools.partial(_expert_kernel, bm=bm, de=de, nb=nb)
    smem_row = lambda off: pl.BlockSpec((None, 1, bm), lambda i, be: (i + off, 0, 0), memory_space=pltpu.SMEM)
    grid_spec = pltpu.PrefetchScalarGridSpec(
        num_scalar_prefetch=1,
        grid=(nb,),
        in_specs=[smem_row(0), smem_row(1), smem_row(0), smem_row(1),
                  pl.BlockSpec(memory_space=pl.ANY),
                  pl.BlockSpec((None, None, d, de2), lambda i, be: (layer, be[i], 0, 0)),
                  pl.BlockSpec((None, None, 1, de2), lambda i, be: (layer, be[i], 0, 0)),
                  pl.BlockSpec((None, None, de, d), lambda i, be: (layer, be[i], 0, 0)),
                  pl.BlockSpec((None, None, 1, d), lambda i, be: (layer, be[i], 0, 0))],
        out_specs=pl.BlockSpec(memory_space=pl.ANY),
        scratch_shapes=[pltpu.VMEM((bm, d), F32), pltpu.VMEM((bm, d), F32),
                        pltpu.VMEM((bm, d), F32), pltpu.VMEM((bm, d), F32),
                        pltpu.SemaphoreType.DMA((2,)), pltpu.SemaphoreType.DMA((2,))],
    )
    src3 = src_ext.reshape(nb + 1, 1, bm)
    dst3 = dst_ext.reshape(nb + 1, 1, bm)
    return pl.pallas_call(
        kern,
        grid_spec=grid_spec,
        out_shape=jax.ShapeDtypeStruct(((nb + 1) * bm, d), F32),
        compiler_params=_params("arbitrary"),
        name="experts",
    )(block_expert, src3, src3, dst3, dst3, h, wgu, bgu.reshape(nl, ne, 1, de2), wd, bd.reshape(nl, ne, 1, d))


def _routing_tables(top_idx, ne, bm):
    t = top_idx.shape[0]
    m = t * TOP_K
    flat_e = top_idx.reshape(m)
    assert ne * m < 2 ** 31
    order = jnp.sort(flat_e * m + jnp.arange(m, dtype=jnp.int32)) % m
    counts = jnp.sum((flat_e[:, None] == jnp.arange(ne, dtype=jnp.int32)[None, :]).astype(jnp.int32), axis=0)
    padded = (counts + bm - 1) // bm * bm
    starts = jnp.cumsum(counts) - counts
    pad_ends = jnp.cumsum(padded)
    pad_starts = pad_ends - padded
    nb = -(-m // bm) + ne
    block_start = jnp.arange(nb, dtype=jnp.int32) * bm
    block_expert = jnp.minimum(jnp.searchsorted(pad_ends, block_start, side='right'), ne - 1).astype(jnp.int32)
    rank = (block_start - pad_starts[block_expert])[:, None] + jnp.arange(bm, dtype=jnp.int32)[None, :]
    valid = (rank >= 0) & (rank < counts[block_expert][:, None])
    assign = order[jnp.clip(starts[block_expert][:, None] + rank, 0, m - 1)]
    src_tok = jnp.where(valid, assign // TOP_K, 0).astype(jnp.int32)
    pad_rank = (jnp.cumsum(jnp.logical_not(valid).reshape(-1).astype(jnp.int32)) - 1).reshape(nb, bm)
    dst_row = jnp.where(valid, (assign % TOP_K) * t + assign // TOP_K, m + pad_rank).astype(jnp.int32)
    first = nb * bm + jnp.arange(bm, dtype=jnp.int32).reshape(1, bm)
    src_ext = jnp.concatenate([src_tok, jnp.zeros((1, bm), jnp.int32)], axis=0)
    dst_ext = jnp.concatenate([first, dst_row], axis=0)
    return block_expert, src_ext, dst_ext


def _combine_kernel(x_ref, y0_ref, y1_ref, y2_ref, y3_ref, tw_ref, gt_ref, fg_ref, o_ref, *, final):
    tw = tw_ref[...]
    acc = tw[:, 0:1] * y0_ref[...]
    for kq, y_ref in enumerate((y1_ref, y2_ref, y3_ref), start=1):
        acc = acc + tw[:, kq:kq + 1] * y_ref[...]
    x = x_ref[...] + gt_ref[...] * acc
    if final:
        ms = jnp.mean(x * x, axis=-1, keepdims=True)
        x = x * lax.rsqrt(ms + RMS_EPS) * fg_ref[...]
    o_ref[...] = x


def _combine(x2, y_rows, top_w, mod3, final_g, seq, final):
    t, d = x2.shape
    assert TOP_K == 4
    tm = _tile(seq, 256)
    nt = t // tm
    bidx = lambda i: (i * tm) // seq
    slot = lambda kq: pl.BlockSpec((tm, d), lambda i: (kq * nt + i, 0))
    kern = functools.partial(_combine_kernel, final=final)
    return pl.pallas_call(
        kern,
        grid=(nt,),
        in_specs=[pl.BlockSpec((tm, d), lambda i: (i, 0)),
                  slot(0), slot(1), slot(2), slot(3),
                  pl.BlockSpec((tm, LANE), lambda i: (i, 0)),
                  pl.BlockSpec((None, 1, d), lambda i: (bidx(i), 0, 5)),
                  pl.BlockSpec((1, d), lambda i: (0, 0))],
        out_specs=pl.BlockSpec((tm, d), lambda i: (i, 0)),
        out_shape=jax.ShapeDtypeStruct((t, d), F32),
        compiler_params=_params("parallel"),
        name="combine",
    )(x2, y_rows, y_rows, y_rows, y_rows, top_w, mod3, final_g.reshape(1, d))


def kernel(x, c, norm1_g, norm2_g, w_mod, b_mod, w_in, conv_w, conv_b, conv_ln_g, conv_ln_b, w_conv_proj, mu_shift, w0, w2, a0, a2, g2, k_k, k_a, r_k, gn_g, gn_b, w_rwkv_proj, v0, v1, mu_v, v2, w_out, router_w, router_b, w_gate_up, b_gate_up, w_down, b_down, final_g):
    bsz, seq, d = x.shape
    depth = w_mod.shape[0]
    ca = conv_w.shape[-1]
    cb = w0.shape[1]
    r_decay, r_aaa, r_gate, r_mv = w2.shape[1], a2.shape[1], g2.shape[1], v1.shape[2]
    ne = router_w.shape[2]
    assert r_decay + r_aaa + r_mv == LORA_W and r_gate == LORA_W
    assert (2 * ca) % cb == 0 and cb == d and ne <= LANE
    t = bsz * seq
    c0 = 2 * ca
    c_lora = c0 + 3 * cb
    c_gate = c_lora + r_decay + r_aaa + r_gate
    col_gate = c0 + 3 * cb
    col_lora = col_gate + 2 * d
    bm = 256

    mod = _modulation(c, w_mod, b_mod)
    wgu_all = w_gate_up.astype(BF)
    wd_all = w_down.astype(BF)
    x2 = x.reshape(t, d)
    v_first = None
    for l in range(depth):
        mod3 = mod[l].reshape(bsz, 1, 6 * d)
        wl = w_in[l]
        if l == 0:
            v1_cols = jnp.zeros((d, r_mv), F32)
            mu_mv = jnp.zeros((r_mv,), F32)
        else:
            v1_cols = v1[l - 1]
            mu_mv = mu_v[l - 1]
        w_all = jnp.concatenate(
            [wl[:, :c_lora], wl[:, c_gate:], wl[:, c_lora:c_lora + r_decay + r_aaa], v1_cols,
             wl[:, c_lora + r_decay + r_aaa:c_gate]], axis=1).astype(BF)
        mu_l = mu_shift[l]
        mu_lora = jnp.concatenate([mu_l[3 * cb:3 * cb + r_decay + r_aaa], mu_mv, mu_l[3 * cb + r_decay + r_aaa:]])
        zpad = lambda top, w, bot: jnp.concatenate(
            [jnp.zeros((top, cb), F32), w, jnp.zeros((bot, cb), F32)], axis=0).astype(BF)
        w2p = zpad(0, w2[l], LORA_W - r_decay)
        a2p = zpad(r_decay, a2[l], r_mv)

        proj = _inproj(x2, mod3, norm1_g[l], w_all, seq)
        a_conv = _conv_branch(proj, conv_w[l], conv_b[l], conv_ln_g[l], conv_ln_b[l], seq)
        vres = None if l == 0 else (zpad(r_decay + r_aaa, v2[l - 1], 0), v0[l - 1], v_first)
        r, ld, k, v, kk, a, g = _rwkv_prep(proj, mu_l[:3 * cb], mu_lora, w2p, a2p, g2[l].astype(BF), w0[l], a0[l],
                                           k_k[l], k_a[l], vres, seq, cb, c0, col_lora, r_decay)
        if l == 0:
            v_first = v
        yg = _wkv_scan(r, ld, k, v, kk, a, g, r_k[l], gn_g[l], gn_b[l], bsz, seq)
        mixed = _mix_branches(a_conv, yg, proj, w_conv_proj[l].astype(BF), w_rwkv_proj[l].astype(BF), col_gate)

        rw = jnp.pad(router_w[l], ((0, 0), (0, LANE - ne)))
        rw_hi = rw.astype(BF)
        rw_lo = (rw - rw_hi.astype(F32)).astype(BF)
        rb = jnp.pad(router_b[l], (0, LANE - ne)).reshape(1, LANE)
        x2, h, top_i, top_w = _out_and_route(mixed, x2, mod3, norm2_g[l], w_out[l].astype(BF), rw_hi, rw_lo, rb,
                                             seq, ne)
        tables = _routing_tables(top_i[:, :TOP_K], ne, bm)
        y_rows = _experts(h, *tables, wgu_all, b_gate_up, wd_all, b_down, l, bm)
        x2 = _combine(x2, y_rows, top_w, mod3, final_g, seq, final=(l == depth - 1))
    return x2.reshape(bsz, seq, d)
```

```python
import functools

import jax
import jax.numpy as jnp
from jax import lax
from jax.experimental import pallas as pl
from jax.experimental.pallas import tpu as pltpu

F32 = jnp.float32
BF = jnp.bfloat16

HEAD = 64
CHUNK = 64
SCAN_LANES = 4 * HEAD
SCAN_GROUPS = 8
TOP_K = 4
RMS_EPS = 1e-5
LN_EPS = 1e-5
GN_EPS = 64e-5
DECAY_SCALE = 0.6065306597126334
SWIGLU_LIMIT = 7.0
SWIGLU_ALPHA = 1.702
LORA_W = 256
LANE = 128
SUBLANE = 8
MXU_ROWS = 256
CONV_HALO = 32
SHIFT_HALO = 16
VMEM_LIMIT = 56 * 1024 * 1024


def _params(*sem):
    return pltpu.CompilerParams(dimension_semantics=sem, vmem_limit_bytes=VMEM_LIMIT)


def _tile(n, pref):
    t = min(n, pref)
    while n % t:
        t -= 8
    return t


def _split2(x):
    hi = x.astype(BF)
    lo = (x - hi.astype(F32)).astype(BF)
    return hi, lo


def _split3(x):
    hi = x.astype(BF)
    r1 = x - hi.astype(F32)
    mid = r1.astype(BF)
    lo = (r1 - mid.astype(F32)).astype(BF)
    return hi, mid, lo


def _dot(a, b):
    return jnp.dot(a, b, preferred_element_type=F32)


def _dot_nt(a, b):
    return lax.dot_general(a, b, (((1,), (1,)), ((), ())), preferred_element_type=F32)


def _sigmoid(x):
    return 1.0 / (1.0 + jnp.exp(-x))


def _mod_kernel(c_ref, w_ref, b_ref, o_ref):
    c = c_ref[...]
    cond = (c * _sigmoid(c)).astype(BF)
    o_ref[...] = _dot(cond, w_ref[...].astype(BF)) + b_ref[...]


def _modulation(c, w_mod, b_mod):
    nl, d, n6 = w_mod.shape
    b = c.shape[0]
    tn = _tile(n6, 1024)
    return pl.pallas_call(
        _mod_kernel,
        grid=(nl, n6 // tn),
        in_specs=[pl.BlockSpec((b, d), lambda l, j: (0, 0)),
                  pl.BlockSpec((None, d, tn), lambda l, j: (l, 0, j)),
                  pl.BlockSpec((None, 1, tn), lambda l, j: (l, 0, j))],
        out_specs=pl.BlockSpec((None, b, tn), lambda l, j: (l, 0, j)),
        out_shape=jax.ShapeDtypeStruct((nl, b, n6), F32),
        compiler_params=_params("parallel", "parallel"),
        name="modulation",
    )(c, w_mod, b_mod.reshape(nl, 1, n6))


def _inproj_kernel(x_ref, sh_ref, sc_ref, g_ref, w_ref, o_ref, u_ref):
    @pl.when(pl.program_id(1) == 0)
    def _():
        x = x_ref[...]
        ms = jnp.mean(x * x, axis=-1, keepdims=True)
        xn = x * lax.rsqrt(ms + RMS_EPS) * g_ref[...]
        u_ref[...] = (xn * (1.0 + sc_ref[...]) + sh_ref[...]).astype(BF)

    tm = u_ref.shape[0]
    rb = min(tm, MXU_ROWS)
    for r0 in range(0, tm, rb):
        o_ref[r0:r0 + rb, :] = _dot(u_ref[r0:r0 + rb, :], w_ref[...]).astype(o_ref.dtype)


def _inproj(x2, mod3, g, w, seq):
    t, d = x2.shape
    n = w.shape[1]
    tm = _tile(seq, 1024)
    tn = _tile(n, 1280)
    bidx = lambda i: (i * tm) // seq
    return pl.pallas_call(
        _inproj_kernel,
        grid=(t // tm, n // tn),
        in_specs=[pl.BlockSpec((tm, d), lambda i, j: (i, 0)),
                  pl.BlockSpec((None, 1, d), lambda i, j: (bidx(i), 0, 0)),
                  pl.BlockSpec((None, 1, d), lambda i, j: (bidx(i), 0, 1)),
                  pl.BlockSpec((1, d), lambda i, j: (0, 0)),
                  pl.BlockSpec((d, tn), lambda i, j: (0, j))],
        out_specs=pl.BlockSpec((tm, tn), lambda i, j: (i, j)),
        out_shape=jax.ShapeDtypeStruct((t, n), BF),
        scratch_shapes=[pltpu.VMEM((tm, d), BF)],
        compiler_params=_params("parallel", "arbitrary"),
        name="inproj",
    )(x2, mod3, mod3, g.reshape(1, d), w)


def _conv_kernel(p_ref, h_ref, w_ref, cb_ref, g_ref, b_ref, o_ref, a_ref, sh_ref, y_ref, *, ts, ca, taps, seq):
    i = pl.program_id(0)
    p = p_ref[...].astype(F32)
    a_ref[pl.ds(CONV_HALO, ts), :] = p[:, :ca] * _sigmoid(p[:, ca:])
    hp = h_ref[...].astype(F32)
    keep = jnp.where((i * ts) % seq == 0, 0.0, 1.0)
    a_ref[pl.ds(0, CONV_HALO), :] = hp[:, :ca] * _sigmoid(hp[:, ca:]) * keep

    rb = min(ts, 64)
    off = CONV_HALO - (taps - 1)
    nrow = ts + CONV_HALO - SUBLANE
    for s in range(1, SUBLANE):
        sh_ref[s - 1, 0:nrow, :] = a_ref[s:s + nrow, :]

    for r0 in range(0, ts, rb):
        for c0 in range(0, ca, LANE):
            acc = jnp.zeros((rb, LANE), F32)
            for j in range(taps):
                s = (off + j) % SUBLANE
                q = r0 + off + j - s
                if s == 0:
                    win = a_ref[q:q + rb, c0:c0 + LANE]
                else:
                    win = sh_ref[s - 1, q:q + rb, c0:c0 + LANE]
                acc = acc + win * w_ref[j:j + 1, c0:c0 + LANE]
            y_ref[r0:r0 + rb, c0:c0 + LANE] = acc
    y = y_ref[...] + cb_ref[...]
    mu = jnp.mean(y, axis=-1, keepdims=True)
    yc = y - mu
    var = jnp.mean(yc * yc, axis=-1, keepdims=True)
    yn = yc * lax.rsqrt(var + LN_EPS) * g_ref[...] + b_ref[...]
    o_ref[...] = (yn * _sigmoid(yn)).astype(o_ref.dtype)


def _conv_branch(proj, conv_w, conv_b, ln_g, ln_b, seq):
    t = proj.shape[0]
    taps, _, ca = conv_w.shape
    assert taps - 1 <= CONV_HALO
    ts = _tile(seq, 256)
    hb = ts // CONV_HALO
    kern = functools.partial(_conv_kernel, ts=ts, ca=ca, taps=taps, seq=seq)
    return pl.pallas_call(
        kern,
        grid=(t // ts,),
        in_specs=[pl.BlockSpec((ts, 2 * ca), lambda i: (i, 0)),
                  pl.BlockSpec((CONV_HALO, 2 * ca), lambda i: (jnp.maximum(i * hb - 1, 0), 0)),
                  pl.BlockSpec((taps, ca), lambda i: (0, 0)),
                  pl.BlockSpec((1, ca), lambda i: (0, 0)),
                  pl.BlockSpec((1, ca), lambda i: (0, 0)),
                  pl.BlockSpec((1, ca), lambda i: (0, 0))],
        out_specs=pl.BlockSpec((ts, ca), lambda i: (i, 0)),
        out_shape=jax.ShapeDtypeStruct((t, ca), BF),
        scratch_shapes=[pltpu.VMEM((ts + CONV_HALO, ca), F32),
                        pltpu.VMEM((SUBLANE - 1, ts + CONV_HALO - SUBLANE, ca), F32),
                        pltpu.VMEM((ts, ca), F32)],
        compiler_params=_params("parallel"),
        name="conv_branch",
    )(proj, proj, conv_w.reshape(taps, ca), conv_b.reshape(1, ca), ln_g.reshape(1, ca), ln_b.reshape(1, ca))


def _token_shift(cur_ref, halo_ref, mu, first):
    cur = cur_ref[...].astype(F32)
    last = halo_ref[SHIFT_HALO - 1:SHIFT_HALO, :].astype(F32)
    last = jnp.where(first, 0.0, last)
    row = lax.broadcasted_iota(jnp.int32, cur.shape, 0)
    prev = jnp.where(row == 0, last, pltpu.roll(cur, 1, axis=0))
    return cur + (prev - cur) * mu


def _prep_kernel(*refs, ts, seq, cb, r_decay, has_vres):
    (r_ref, k_ref, v_ref, l_ref, rh_ref, kh_ref, vh_ref, lh_ref, mur_ref, muk_ref, muv_ref, mul_ref,
     w2_ref, a2_ref, g2_ref, w0_ref, a0_ref, kk_ref, ka_ref, e_ref, et_ref) = refs[:21]
    rest = refs[21:]
    if has_vres:
        v2_ref, v0_ref, vf_ref = rest[:3]
        rest = rest[3:]
    ro_ref, ldo_ref, ko_ref, vo_ref, kko_ref, ao_ref, go_ref = rest

    first = (pl.program_id(0) * ts) % seq == 0
    r = _token_shift(r_ref, rh_ref, mur_ref[...], first)
    k = _token_shift(k_ref, kh_ref, muk_ref[...], first)
    v = _token_shift(v_ref, vh_ref, muv_ref[...], first)
    lz = _token_shift(l_ref, lh_ref, mul_ref[...], first)

    l1 = lz[:, :LORA_W]
    lane = lax.broadcasted_iota(jnp.int32, l1.shape, 1)
    t1 = jnp.where(lane < r_decay, jnp.tanh(l1), l1).astype(BF)
    wpre = w0_ref[...] + _dot(t1, w2_ref[...])
    ldo_ref[...] = -DECAY_SCALE * _sigmoid(wpre)
    a = _sigmoid(a0_ref[...] + _dot(t1, a2_ref[...]))
    go_ref[...] = _dot(_sigmoid(lz[:, LORA_W:]).astype(BF), g2_ref[...]).astype(go_ref.dtype)
    if has_vres:
        v = v + (vf_ref[...].astype(F32) - v) * _sigmoid(v0_ref[...] + _dot(t1, v2_ref[...]))

    kk = k * kk_ref[...]
    sq_hi, sq_lo = _split2(kk * kk)
    ss = _dot(sq_hi, e_ref[...]) + _dot(sq_lo, e_ref[...])
    inv = lax.rsqrt(jnp.maximum(ss, 1e-24))
    inv_hi, inv_mid, inv_lo = _split3(inv)
    invb = _dot(inv_hi, et_ref[...]) + _dot(inv_mid, et_ref[...]) + _dot(inv_lo, et_ref[...])
    ro_ref[...] = r.astype(ro_ref.dtype)
    ko_ref[...] = (k * (1.0 + (a - 1.0) * ka_ref[...])).astype(ko_ref.dtype)
    vo_ref[...] = v.astype(vo_ref.dtype)
    kko_ref[...] = (kk * invb).astype(kko_ref.dtype)
    ao_ref[...] = a.astype(ao_ref.dtype)


def _rwkv_prep(proj, mu_rkv, mu_lora, w2p, a2p, g2, w0, a0, k_k, k_a, vres, seq, cb, col_r, col_lora, r_decay):
    t = proj.shape[0]
    ts = _tile(seq, 256)
    hb = ts // SHIFT_HALO
    nh = cb // HEAD
    hp = max(nh, LANE)
    head_of = jnp.arange(cb, dtype=jnp.int32) // HEAD
    e = (head_of[:, None] == jnp.arange(hp, dtype=jnp.int32)[None, :]).astype(BF)
    et = e.T
    lw = 2 * LORA_W
    cr = col_r // cb
    cl = col_lora // lw
    halo = lambda i: jnp.maximum(i * hb - 1, 0)
    row = lambda n: pl.BlockSpec((1, n), lambda i: (0, 0))
    full = lambda a, b: pl.BlockSpec((a, b), lambda i: (0, 0))
    in_specs = [pl.BlockSpec((ts, cb), lambda i: (i, cr)),
                pl.BlockSpec((ts, cb), lambda i: (i, cr + 1)),
                pl.BlockSpec((ts, cb), lambda i: (i, cr + 2)),
                pl.BlockSpec((ts, lw), lambda i: (i, cl)),
                pl.BlockSpec((SHIFT_HALO, cb), lambda i: (halo(i), cr)),
                pl.BlockSpec((SHIFT_HALO, cb), lambda i: (halo(i), cr + 1)),
                pl.BlockSpec((SHIFT_HALO, cb), lambda i: (halo(i), cr + 2)),
                pl.BlockSpec((SHIFT_HALO, lw), lambda i: (halo(i), cl)),
                row(cb), row(cb), row(cb), row(lw),
                full(LORA_W, cb), full(LORA_W, cb), full(LORA_W, cb),
                row(cb), row(cb), row(cb), row(cb), full(cb, hp), full(hp, cb)]
    args = [proj, proj, proj, proj, proj, proj, proj, proj,
            mu_rkv[0:cb].reshape(1, cb), mu_rkv[cb:2 * cb].reshape(1, cb), mu_rkv[2 * cb:3 * cb].reshape(1, cb),
            mu_lora.reshape(1, lw), w2p, a2p, g2, w0.reshape(1, cb), a0.reshape(1, cb),
            k_k.reshape(1, cb), k_a.reshape(1, cb), e, et]
    if vres is not None:
        v2p, v0, v_first = vres
        in_specs += [full(LORA_W, cb), row(cb), pl.BlockSpec((ts, cb), lambda i: (i, 0))]
        args += [v2p, v0.reshape(1, cb), v_first]
    tile = pl.BlockSpec((ts, cb), lambda i: (i, 0))
    kern = functools.partial(_prep_kernel, ts=ts, seq=seq, cb=cb, r_decay=r_decay, has_vres=vres is not None)
    out = lambda dt: jax.ShapeDtypeStruct((t, cb), dt)
    return pl.pallas_call(
        kern,
        grid=(t // ts,),
        in_specs=in_specs,
        out_specs=[tile] * 7,
        out_shape=[out(BF), out(F32), out(BF), out(BF), out(BF), out(BF), out(BF)],
        compiler_params=_params("parallel"),
        name="rwkv_prep",
    )(*args)


def _scan_kernel(r_ref, ld_ref, k_ref, v_ref, kk_ref, a_ref, g_ref, rk_ref, gg_ref, gb_ref, o_ref, s_ref,
                 *, nchunk, ng):
    c = CHUNK
    lg = SCAN_LANES
    nq = lg // HEAD
    groups = range(ng)

    @pl.when(pl.program_id(2) == 0)
    def _():
        s_ref[...] = jnp.zeros_like(s_ref)

    row = lax.broadcasted_iota(jnp.int32, (lg, lg), 0)
    col = lax.broadcasted_iota(jnp.int32, (lg, lg), 1)
    same = (row // c) == (col // HEAD)
    bd = same.astype(F32)
    strict = same & (row > col)
    incl = same & (row >= col)
    eye = (row == col).astype(F32)
    avg = (bd * (1.0 / HEAD)).astype(BF)
    ones = bd.astype(BF)
    tr = lax.broadcasted_iota(jnp.int32, (c, c), 0)
    tc = lax.broadcasted_iota(jnp.int32, (c, c), 1)
    ltri = (tr >= tc).astype(BF)

    def stack(x):
        return jnp.concatenate([x] * nq, axis=0) * bd

    def unstack(y):
        out = y[0:c]
        for qi in range(1, nq):
            out = out + y[qi * c:(qi + 1) * c]
        return out

    def body(ci, carry):
        sl = pl.ds(pl.multiple_of(ci * c, c), c)
        ld = ld_ref[sl, :]
        l_hi, l_mid, l_lo = _split3(ld)
        cum = _dot(ltri, l_hi) + _dot(ltri, l_mid) + _dot(ltri, l_lo)
        dinc = jnp.exp(cum)
        dinv = jnp.exp(-cum)
        dexc = jnp.exp(cum - ld)
        kk = kk_ref[sl, :].astype(F32)
        r = r_ref[sl, :].astype(F32)
        k = k_ref[sl, :].astype(F32)
        v = v_ref[sl, :].astype(F32)
        at = -kk * dexc
        rt = r * dinc
        bt = kk * a_ref[sl, :].astype(F32) * dinv
        kt = k * dinv
        dtot = dinc[c - 1:c, :]
        lanes = [slice(gi * lg, (gi + 1) * lg) for gi in groups]

        r_s = [stack(rt[:, ln]) for ln in lanes]
        b_s = [stack(bt[:, ln]) for ln in lanes]
        k_s = [stack(kt[:, ln]) for ln in lanes]
        a_b = [stack(at[:, ln]).astype(BF) for ln in lanes]
        v_b = [stack(v[:, ln]).astype(BF) for ln in lanes]
        r_b = [x.astype(BF) for x in r_s]
        b_b = [x.astype(BF) for x in b_s]
        k_b = [x.astype(BF) for x in k_s]

        a_ab = [jnp.where(strict, _dot_nt(a_b[gi], b_b[gi]), 0.0) for gi in groups]
        a_ak = [jnp.where(strict, _dot_nt(a_b[gi], k_b[gi]), 0.0).astype(BF) for gi in groups]
        a_rb = [jnp.where(incl, _dot_nt(r_b[gi], b_b[gi]), 0.0).astype(BF) for gi in groups]
        a_rk = [jnp.where(incl, _dot_nt(r_b[gi], k_b[gi]), 0.0).astype(BF) for gi in groups]

        tm = [eye + x for x in a_ab]
        pw = a_ab
        for _ in range(c.bit_length() - 2):
            pb = [x.astype(BF) for x in pw]
            pw = [_dot(x, x) for x in pb]
            tm = [tm[gi] + _dot(tm[gi].astype(BF), pw[gi].astype(BF)) for gi in groups]
        tmb = [x.astype(BF) for x in tm]

        akv = [_dot(a_ak[gi], v_b[gi]).astype(BF) for gi in groups]
        wb = [_dot(tmb[gi], a_b[gi]).astype(BF) for gi in groups]
        u0b = [_dot(tmb[gi], akv[gi]).astype(BF) for gi in groups]
        q = [unstack(r_s[gi] + _dot(a_rb[gi], wb[gi])).astype(BF) for gi in groups]
        y0 = [unstack(_dot(a_rb[gi], u0b[gi]) + _dot(a_rk[gi], v_b[gi])) for gi in groups]
        bd_t = [(b_s[gi] * dtot[:, lanes[gi]]).T.astype(BF) for gi in groups]
        kd_t = [(k_s[gi] * dtot[:, lanes[gi]]).T.astype(BF) for gi in groups]
        m = [(eye * dtot[:, lanes[gi]] + _dot(bd_t[gi], wb[gi])).astype(BF) for gi in groups]
        pm = [_dot(bd_t[gi], u0b[gi]) + _dot(kd_t[gi], v_b[gi]) for gi in groups]

        s0 = [s_ref[gi].astype(BF) for gi in groups]
        yt = [_dot(q[gi], s0[gi]) + y0[gi] for gi in groups]
        for gi in groups:
            s_ref[gi] = _dot(m[gi], s0[gi]) + pm[gi]

        for gi in groups:
            ln = lanes[gi]
            mu = _dot(yt[gi].astype(BF), avg)
            yc = yt[gi] - mu
            var = _dot((yc * yc).astype(BF), avg)
            yn = yc * lax.rsqrt(var + GN_EPS) * gg_ref[:, ln] + gb_ref[:, ln]
            bonus = _dot((r[:, ln] * k[:, ln] * rk_ref[:, ln]).astype(BF), ones)
            o_ref[sl, ln] = ((yn + bonus * v[:, ln]) * g_ref[sl, ln].astype(F32)).astype(o_ref.dtype)
        return carry

    lax.fori_loop(0, nchunk, body, 0)


def _wkv_scan(r, ld, k, v, kk, a, g, r_k, gn_g, gn_b, bsz, seq):
    t, cb = r.shape
    assert cb % SCAN_LANES == 0
    ng = min(cb // SCAN_LANES, SCAN_GROUPS)
    lb = ng * SCAN_LANES
    sb = _tile(seq, 256)
    nsb = seq // sb
    tile = pl.BlockSpec((sb, lb), lambda b, h, s: (b * nsb + s, h))
    vec = pl.BlockSpec((1, lb), lambda b, h, s: (0, h))
    kern = functools.partial(_scan_kernel, nchunk=sb // CHUNK, ng=ng)
    return pl.pallas_call(
        kern,
        grid=(bsz, cb // lb, nsb),
        in_specs=[tile] * 7 + [vec] * 3,
        out_specs=tile,
        out_shape=jax.ShapeDtypeStruct((t, cb), BF),
        scratch_shapes=[pltpu.VMEM((ng, SCAN_LANES, SCAN_LANES), F32)],
        compiler_params=_params("parallel", "parallel", "arbitrary"),
        name="wkv_scan",
    )(r, ld, k, v, kk, a, g, r_k.reshape(1, cb), gn_g.reshape(1, cb), gn_b.reshape(1, cb))


def _mixa_kernel(a_ref, y_ref, ga_ref, gb_ref, wc_ref, wr_ref, o_ref):
    ya = _dot(a_ref[...], wc_ref[...])
    yb = _dot(y_ref[...], wr_ref[...])
    m = _sigmoid(ga_ref[...].astype(F32)) * ya + _sigmoid(gb_ref[...].astype(F32)) * yb
    o_ref[...] = m.astype(o_ref.dtype)


def _mix_branches(a_conv, yg, proj, wc, wr, col_gate):
    t, ca = a_conv.shape
    cb = yg.shape[1]
    d = wc.shape[1]
    tm = _tile(t, 512)
    cg = col_gate // d
    return pl.pallas_call(
        _mixa_kernel,
        grid=(t // tm,),
        in_specs=[pl.BlockSpec((tm, ca), lambda i: (i, 0)),
                  pl.BlockSpec((tm, cb), lambda i: (i, 0)),
                  pl.BlockSpec((tm, d), lambda i: (i, cg)),
                  pl.BlockSpec((tm, d), lambda i: (i, cg + 1)),
                  pl.BlockSpec((ca, d), lambda i: (0, 0)),
                  pl.BlockSpec((cb, d), lambda i: (0, 0))],
        out_specs=pl.BlockSpec((tm, d), lambda i: (i, 0)),
        out_shape=jax.ShapeDtypeStruct((t, d), BF),
        compiler_params=_params("parallel"),
        name="mix_branches",
    )(a_conv, yg, proj, proj, wc, wr)


def _mixb_kernel(m_ref, x_ref, gt_ref, sh_ref, sc_ref, g_ref, wo_ref, rwh_ref, rwl_ref, rb_ref,
                 xo_ref, ho_ref, ti_ref, tw_ref, *, ne):
    x = x_ref[...] + gt_ref[...] * _dot(m_ref[...], wo_ref[...])
    xo_ref[...] = x
    ms = jnp.mean(x * x, axis=-1, keepdims=True)
    h = x * lax.rsqrt(ms + RMS_EPS) * g_ref[...] * (1.0 + sc_ref[...]) + sh_ref[...]
    ho_ref[...] = h
    h_hi, h_lo = _split2(h)
    logits = _dot(h_hi, rwh_ref[...]) + _dot(h_lo, rwh_ref[...]) + _dot(h_hi, rwl_ref[...]) + rb_ref[...]
    lane = lax.broadcasted_iota(jnp.int32, logits.shape, 1)
    neg = jnp.float32(-jnp.inf)
    cur = jnp.where(lane < ne, logits, neg)
    vals, idxs = [], []
    for _ in range(TOP_K):
        mx = jnp.max(cur, axis=-1, keepdims=True)
        ix = jnp.min(jnp.where(cur == mx, lane, LANE), axis=-1, keepdims=True)
        vals.append(mx)
        idxs.append(ix)
        cur = jnp.where(lane == ix, neg, cur)
    es = [jnp.exp(vk - vals[0]) for vk in vals]
    den = es[0]
    for ek in es[1:]:
        den = den + ek
    ti = jnp.zeros(logits.shape, jnp.int32)
    tw = jnp.zeros(logits.shape, F32)
    for kq in range(TOP_K):
        ti = jnp.where(lane == kq, idxs[kq], ti)
        tw = jnp.where(lane == kq, es[kq] / den, tw)
    ti_ref[...] = ti
    tw_ref[...] = tw


def _out_and_route(m, x2, mod3, g2, wo, rw_hi, rw_lo, rb, seq, ne):
    t, d = x2.shape
    tm = _tile(seq, 256)
    bidx = lambda i: (i * tm) // seq
    modv = lambda which: pl.BlockSpec((None, 1, d), lambda i: (bidx(i), 0, which))
    kern = functools.partial(_mixb_kernel, ne=ne)
    return pl.pallas_call(
        kern,
        grid=(t // tm,),
        in_specs=[pl.BlockSpec((tm, d), lambda i: (i, 0)),
                  pl.BlockSpec((tm, d), lambda i: (i, 0)),
                  modv(2), modv(3), modv(4),
                  pl.BlockSpec((1, d), lambda i: (0, 0)),
                  pl.BlockSpec((d, d), lambda i: (0, 0)),
                  pl.BlockSpec((d, LANE), lambda i: (0, 0)),
                  pl.BlockSpec((d, LANE), lambda i: (0, 0)),
                  pl.BlockSpec((1, LANE), lambda i: (0, 0))],
        out_specs=[pl.BlockSpec((tm, d), lambda i: (i, 0)),
                   pl.BlockSpec((tm, d), lambda i: (i, 0)),
                   pl.BlockSpec((tm, LANE), lambda i: (i, 0)),
                   pl.BlockSpec((tm, LANE), lambda i: (i, 0))],
        out_shape=[jax.ShapeDtypeStruct((t, d), F32), jax.ShapeDtypeStruct((t, d), F32),
                   jax.ShapeDtypeStruct((t, LANE), jnp.int32), jax.ShapeDtypeStruct((t, LANE), F32)],
        compiler_params=_params("parallel"),
        name="out_and_route",
    )(m, x2, mod3, mod3, mod3, g2.reshape(1, d), wo, rw_hi, rw_lo, rb)


def _expert_kernel(be_ref, src_cur, src_nxt, h_hbm, wgu_ref, bgu_ref, wd_ref, bd_ref, o_ref,
                   x0, x1, gsem, *, bm, de, nb):
    i = pl.program_id(0)

    def start_gather(src_ref, xb, sem):
        for rw in range(bm):
            pltpu.make_async_copy(h_hbm.at[pl.ds(src_ref[0, rw], 1)], xb.at[pl.ds(rw, 1)], sem).start()

    def wait_gather(xb, sem):
        pltpu.make_async_copy(h_hbm.at[pl.ds(0, bm)], xb, sem).wait()

    def step(xc, xo, gc, go):
        @pl.when(i == 0)
        def _():
            start_gather(src_cur, xc, gc)

        wait_gather(xc, gc)

        @pl.when(be_ref[i] >= 0)
        def _():
            start_gather(src_nxt, xo, go)

        x = xc[...].astype(BF)
        gu = _dot(x, wgu_ref[...]) + bgu_ref[...]
        gate = jnp.minimum(gu[:, :de], SWIGLU_LIMIT)
        up = jnp.clip(gu[:, de:], -SWIGLU_LIMIT, SWIGLU_LIMIT)
        act = ((up + 1.0) * gate * _sigmoid(SWIGLU_ALPHA * gate)).astype(BF)
        o_ref[...] = _dot(act, wd_ref[...]) + bd_ref[...]

        @pl.when(i == nb - 1)
        def _():
            wait_gather(xo, go)

    @pl.when(i % 2 == 0)
    def _():
        step(x0, x1, gsem.at[0], gsem.at[1])

    @pl.when(i % 2 == 1)
    def _():
        step(x1, x0, gsem.at[1], gsem.at[0])


def _experts(h, block_expert, src_ext, wgu, bgu, wd, bd, layer, bm):
    t, d = h.shape
    nl, ne, _, de2 = wgu.shape
    de = de2 // 2
    nb = block_expert.shape[0]
    kern = functools.partial(_expert_kernel, bm=bm, de=de, nb=nb)
    smem_row = lambda off: pl.BlockSpec((None, 1, bm), lambda i, be: (i + off, 0, 0), memory_space=pltpu.SMEM)
    grid_spec = pltpu.PrefetchScalarGridSpec(
        num_scalar_prefetch=1,
        grid=(nb,),
        in_specs=[smem_row(0), smem_row(1),
                  pl.BlockSpec(memory_space=pl.ANY),
                  pl.BlockSpec((None, None, d, de2), lambda i, be: (layer, be[i], 0, 0)),
                  pl.BlockSpec((None, None, 1, de2), lambda i, be: (layer, be[i], 0, 0)),
                  pl.BlockSpec((None, None, de, d), lambda i, be: (layer, be[i], 0, 0)),
                  pl.BlockSpec((None, None, 1, d), lambda i, be: (layer, be[i], 0, 0))],
        out_specs=pl.BlockSpec((bm, d), lambda i, be: (i, 0)),
        scratch_shapes=[pltpu.VMEM((bm, d), F32), pltpu.VMEM((bm, d), F32), pltpu.SemaphoreType.DMA((2,))],
    )
    src3 = src_ext.reshape(nb + 1, 1, bm)
    return pl.pallas_call(
        kern,
        grid_spec=grid_spec,
        out_shape=jax.ShapeDtypeStruct((nb * bm, d), F32),
        compiler_params=_params("arbitrary"),
        name="experts",
    )(block_expert, src3, src3, h, wgu, bgu.reshape(nl, ne, 1, de2), wd, bd.reshape(nl, ne, 1, d))


def _routing_tables(top_idx, ne, bm, tm):
    t = top_idx.shape[0]
    m = t * TOP_K
    flat_e = top_idx.reshape(m)
    assert ne * m < 2 ** 31
    order = jnp.sort(flat_e * m + jnp.arange(m, dtype=jnp.int32)) % m
    counts = jnp.sum((flat_e[:, None] == jnp.arange(ne, dtype=jnp.int32)[None, :]).astype(jnp.int32), axis=0)
    padded = (counts + bm - 1) // bm * bm
    starts = jnp.cumsum(counts) - counts
    pad_ends = jnp.cumsum(padded)
    pad_starts = pad_ends - padded
    nb = -(-m // bm) + ne
    block_start = jnp.arange(nb, dtype=jnp.int32) * bm
    block_expert = jnp.sum((pad_ends[None, :] <= block_start[:, None]).astype(jnp.int32), axis=1)
    block_expert = jnp.minimum(block_expert, ne - 1)
    rank = (block_start - pad_starts[block_expert])[:, None] + jnp.arange(bm, dtype=jnp.int32)[None, :]
    valid = (rank >= 0) & (rank < counts[block_expert][:, None])
    assign = order[jnp.clip(starts[block_expert][:, None] + rank, 0, m - 1)]
    src_tok = jnp.where(valid, assign // TOP_K, 0).astype(jnp.int32)
    src_ext = jnp.concatenate([src_tok, jnp.zeros((1, bm), jnp.int32)], axis=0)
    inv = jnp.argsort(order).astype(jnp.int32)
    pos = (pad_starts - starts)[flat_e] + inv
    pos = pos.reshape(t // tm, tm, TOP_K).transpose(0, 2, 1).reshape(t // tm, TOP_K * tm)
    pos_ext = jnp.concatenate([pos, jnp.zeros((1, TOP_K * tm), jnp.int32)], axis=0).astype(jnp.int32)
    return block_expert, src_ext, pos_ext


def _combine_kernel(pos_cur, pos_nxt, x_ref, tw_ref, gt_ref, fg_ref, y_hbm, o_ref, y0, y1, gsem,
                    *, tm, nt, final):
    i = pl.program_id(0)
    nrow = TOP_K * tm

    def start_gather(pos_ref, yb, sem):
        for rw in range(nrow):
            pltpu.make_async_copy(y_hbm.at[pl.ds(pos_ref[0, rw], 1)], yb.at[pl.ds(rw, 1)], sem).start()

    def wait_gather(yb, sem):
        pltpu.make_async_copy(y_hbm.at[pl.ds(0, nrow)], yb, sem).wait()

    def step(yc, yo, gc, go):
        @pl.when(i == 0)
        def _():
            start_gather(pos_cur, yc, gc)

        wait_gather(yc, gc)

        @pl.when(pos_nxt[0, 0] >= 0)
        def _():
            start_gather(pos_nxt, yo, go)

        tw = tw_ref[...]
        acc = tw[:, 0:1] * yc[0:tm, :]
        for kq in range(1, TOP_K):
            acc = acc + tw[:, kq:kq + 1] * yc[kq * tm:(kq + 1) * tm, :]
        x = x_ref[...] + gt_ref[...] * acc
        if final:
            ms = jnp.mean(x * x, axis=-1, keepdims=True)
            x = x * lax.rsqrt(ms + RMS_EPS) * fg_ref[...]
        o_ref[...] = x

        @pl.when(i == nt - 1)
        def _():
            wait_gather(yo, go)

    @pl.when(i % 2 == 0)
    def _():
        step(y0, y1, gsem.at[0], gsem.at[1])

    @pl.when(i % 2 == 1)
    def _():
        step(y1, y0, gsem.at[1], gsem.at[0])


def _combine(x2, y_sorted, pos_ext, top_w, mod3, final_g, seq, tm, final):
    t, d = x2.shape
    nt = t // tm
    bidx = lambda i: (i * tm) // seq
    kern = functools.partial(_combine_kernel, tm=tm, nt=nt, final=final)
    smem_row = lambda off: pl.BlockSpec((None, 1, TOP_K * tm), lambda i: (i + off, 0, 0), memory_space=pltpu.SMEM)
    pos3 = pos_ext.reshape(nt + 1, 1, TOP_K * tm)
    return pl.pallas_call(
        kern,
        grid=(nt,),
        in_specs=[smem_row(0), smem_row(1),
                  pl.BlockSpec((tm, d), lambda i: (i, 0)),
                  pl.BlockSpec((tm, LANE), lambda i: (i, 0)),
                  pl.BlockSpec((None, 1, d), lambda i: (bidx(i), 0, 5)),
                  pl.BlockSpec((1, d), lambda i: (0, 0)),
                  pl.BlockSpec(memory_space=pl.ANY)],
        out_specs=pl.BlockSpec((tm, d), lambda i: (i, 0)),
        out_shape=jax.ShapeDtypeStruct((t, d), F32),
        scratch_shapes=[pltpu.VMEM((TOP_K * tm, d), F32), pltpu.VMEM((TOP_K * tm, d), F32),
                        pltpu.SemaphoreType.DMA((2,))],
        compiler_params=_params("arbitrary"),
        name="combine",
    )(pos3, pos3, x2, top_w, mod3, final_g.reshape(1, d), y_sorted)


def kernel(x, c, norm1_g, norm2_g, w_mod, b_mod, w_in, conv_w, conv_b, conv_ln_g, conv_ln_b, w_conv_proj, mu_shift, w0, w2, a0, a2, g2, k_k, k_a, r_k, gn_g, gn_b, w_rwkv_proj, v0, v1, mu_v, v2, w_out, router_w, router_b, w_gate_up, b_gate_up, w_down, b_down, final_g):
    bsz, seq, d = x.shape
    depth = w_mod.shape[0]
    ca = conv_w.shape[-1]
    cb = w0.shape[1]
    r_decay, r_aaa, r_gate, r_mv = w2.shape[1], a2.shape[1], g2.shape[1], v1.shape[2]
    ne = router_w.shape[2]
    assert r_decay + r_aaa + r_mv == LORA_W and r_gate == LORA_W
    assert (2 * ca) % cb == 0 and cb == d and ne <= LANE
    t = bsz * seq
    c0 = 2 * ca
    c_lora = c0 + 3 * cb
    c_gate = c_lora + r_decay + r_aaa + r_gate
    col_gate = c0 + 3 * cb
    col_lora = col_gate + 2 * d
    bm = 256
    tmc = _tile(seq, 256)

    mod = _modulation(c, w_mod, b_mod)
    wgu_all = w_gate_up.astype(BF)
    wd_all = w_down.astype(BF)
    x2 = x.reshape(t, d)
    v_first = None
    for l in range(depth):
        mod3 = mod[l].reshape(bsz, 1, 6 * d)
        wl = w_in[l]
        if l == 0:
            v1_cols = jnp.zeros((d, r_mv), F32)
            mu_mv = jnp.zeros((r_mv,), F32)
        else:
            v1_cols = v1[l - 1]
            mu_mv = mu_v[l - 1]
        w_all = jnp.concatenate(
            [wl[:, :c_lora], wl[:, c_gate:], wl[:, c_lora:c_lora + r_decay + r_aaa], v1_cols,
             wl[:, c_lora + r_decay + r_aaa:c_gate]], axis=1).astype(BF)
        mu_l = mu_shift[l]
        mu_lora = jnp.concatenate([mu_l[3 * cb:3 * cb + r_decay + r_aaa], mu_mv, mu_l[3 * cb + r_decay + r_aaa:]])
        zpad = lambda top, w, bot: jnp.concatenate(
            [jnp.zeros((top, cb), F32), w, jnp.zeros((bot, cb), F32)], axis=0).astype(BF)
        w2p = zpad(0, w2[l], LORA_W - r_decay)
        a2p = zpad(r_decay, a2[l], r_mv)

        proj = _inproj(x2, mod3, norm1_g[l], w_all, seq)
        a_conv = _conv_branch(proj, conv_w[l], conv_b[l], conv_ln_g[l], conv_ln_b[l], seq)
        vres = None if l == 0 else (zpad(r_decay + r_aaa, v2[l - 1], 0), v0[l - 1], v_first)
        r, ld, k, v, kk, a, g = _rwkv_prep(proj, mu_l[:3 * cb], mu_lora, w2p, a2p, g2[l].astype(BF), w0[l], a0[l],
                                           k_k[l], k_a[l], vres, seq, cb, c0, col_lora, r_decay)
        if l == 0:
            v_first = v
        yg = _wkv_scan(r, ld, k, v, kk, a, g, r_k[l], gn_g[l], gn_b[l], bsz, seq)
        mixed = _mix_branches(a_conv, yg, proj, w_conv_proj[l].astype(BF), w_rwkv_proj[l].astype(BF), col_gate)

        rw = jnp.pad(router_w[l], ((0, 0), (0, LANE - ne)))
        rw_hi = rw.astype(BF)
        rw_lo = (rw - rw_hi.astype(F32)).astype(BF)
        rb = jnp.pad(router_b[l], (0, LANE - ne)).reshape(1, LANE)
        x2, h, top_i, top_w = _out_and_route(mixed, x2, mod3, norm2_g[l], w_out[l].astype(BF), rw_hi, rw_lo, rb,
                                             seq, ne)
        block_expert, src_ext, pos_ext = _routing_tables(top_i[:, :TOP_K], ne, bm, tmc)
        y_sorted = _experts(h, block_expert, src_ext, wgu_all, b_gate_up, wd_all, b_down, l, bm)
        x2 = _combine(x2, y_sorted, pos_ext, top_w, mod3, final_g, seq, tmc, final=(l == depth - 1))
    return x2.reshape(bsz, seq, d)
```

```python
import functools

import jax
import jax.numpy as jnp
from jax import lax
from jax.experimental import pallas as pl
from jax.experimental.pallas import tpu as pltpu

F32 = jnp.float32
BF = jnp.bfloat16

HEAD = 64
CHUNK = 64
SCAN_LANES = 4 * HEAD
SCAN_GROUPS = 8
TOP_K = 4
RMS_EPS = 1e-5
LN_EPS = 1e-5
GN_EPS = 64e-5
DECAY_SCALE = 0.6065306597126334
SWIGLU_LIMIT = 7.0
SWIGLU_ALPHA = 1.702
LORA_W = 256
LANE = 128
SUBLANE = 8
MXU_ROWS = 256
CONV_HALO = 32
SHIFT_HALO = 16
VMEM_LIMIT = 56 * 1024 * 1024


def _params(*sem):
    return pltpu.CompilerParams(dimension_semantics=sem, vmem_limit_bytes=VMEM_LIMIT)


def _tile(n, pref):
    t = min(n, pref)
    while n % t:
        t -= 8
    return t


def _split2(x):
    hi = x.astype(BF)
    lo = (x - hi.astype(F32)).astype(BF)
    return hi, lo


def _split3(x):
    hi = x.astype(BF)
    r1 = x - hi.astype(F32)
    mid = r1.astype(BF)
    lo = (r1 - mid.astype(F32)).astype(BF)
    return hi, mid, lo


def _dot(a, b):
    return jnp.dot(a, b, preferred_element_type=F32)


def _dot_nt(a, b):
    return lax.dot_general(a, b, (((1,), (1,)), ((), ())), preferred_element_type=F32)


def _sigmoid(x):
    return 1.0 / (1.0 + jnp.exp(-x))


def _mod_kernel(c_ref, w_ref, b_ref, o_ref):
    c = c_ref[...]
    cond = (c * _sigmoid(c)).astype(BF)
    o_ref[...] = _dot(cond, w_ref[...].astype(BF)) + b_ref[...]


def _modulation(c, w_mod, b_mod):
    nl, d, n6 = w_mod.shape
    b = c.shape[0]
    tn = _tile(n6, 1024)
    return pl.pallas_call(
        _mod_kernel,
        grid=(nl, n6 // tn),
        in_specs=[pl.BlockSpec((b, d), lambda l, j: (0, 0)),
                  pl.BlockSpec((None, d, tn), lambda l, j: (l, 0, j)),
                  pl.BlockSpec((None, 1, tn), lambda l, j: (l, 0, j))],
        out_specs=pl.BlockSpec((None, b, tn), lambda l, j: (l, 0, j)),
        out_shape=jax.ShapeDtypeStruct((nl, b, n6), F32),
        compiler_params=_params("parallel", "parallel"),
        name="modulation",
    )(c, w_mod, b_mod.reshape(nl, 1, n6))


def _inproj_kernel(x_ref, sh_ref, sc_ref, g_ref, w_ref, o_ref, u_ref):
    @pl.when(pl.program_id(1) == 0)
    def _():
        x = x_ref[...]
        ms = jnp.mean(x * x, axis=-1, keepdims=True)
        xn = x * lax.rsqrt(ms + RMS_EPS) * g_ref[...]
        u_ref[...] = (xn * (1.0 + sc_ref[...]) + sh_ref[...]).astype(BF)

    tm = u_ref.shape[0]
    rb = min(tm, MXU_ROWS)
    for r0 in range(0, tm, rb):
        o_ref[r0:r0 + rb, :] = _dot(u_ref[r0:r0 + rb, :], w_ref[...]).astype(o_ref.dtype)


def _inproj(x2, mod3, g, w, seq):
    t, d = x2.shape
    n = w.shape[1]
    tm = _tile(seq, 1024)
    tn = _tile(n, 1280)
    bidx = lambda i: (i * tm) // seq
    return pl.pallas_call(
        _inproj_kernel,
        grid=(t // tm, n // tn),
        in_specs=[pl.BlockSpec((tm, d), lambda i, j: (i, 0)),
                  pl.BlockSpec((None, 1, d), lambda i, j: (bidx(i), 0, 0)),
                  pl.BlockSpec((None, 1, d), lambda i, j: (bidx(i), 0, 1)),
                  pl.BlockSpec((1, d), lambda i, j: (0, 0)),
                  pl.BlockSpec((d, tn), lambda i, j: (0, j))],
        out_specs=pl.BlockSpec((tm, tn), lambda i, j: (i, j)),
        out_shape=jax.ShapeDtypeStruct((t, n), BF),
        scratch_shapes=[pltpu.VMEM((tm, d), BF)],
        compiler_params=_params("parallel", "arbitrary"),
        name="inproj",
    )(x2, mod3, mod3, g.reshape(1, d), w)


def _conv_kernel(p_ref, h_ref, w_ref, cb_ref, g_ref, b_ref, o_ref, a_ref, sh_ref, y_ref, *, ts, ca, taps, seq):
    i = pl.program_id(0)
    p = p_ref[...].astype(F32)
    a_ref[pl.ds(CONV_HALO, ts), :] = p[:, :ca] * _sigmoid(p[:, ca:])
    hp = h_ref[...].astype(F32)
    keep = jnp.where((i * ts) % seq == 0, 0.0, 1.0)
    a_ref[pl.ds(0, CONV_HALO), :] = hp[:, :ca] * _sigmoid(hp[:, ca:]) * keep

    rb = min(ts, 64)
    off = CONV_HALO - (taps - 1)
    nrow = ts + CONV_HALO - SUBLANE
    for s in range(1, SUBLANE):
        sh_ref[s - 1, 0:nrow, :] = a_ref[s:s + nrow, :]

    for r0 in range(0, ts, rb):
        for c0 in range(0, ca, LANE):
            acc = jnp.zeros((rb, LANE), F32)
            for j in range(taps):
                s = (off + j) % SUBLANE
                q = r0 + off + j - s
                if s == 0:
                    win = a_ref[q:q + rb, c0:c0 + LANE]
                else:
                    win = sh_ref[s - 1, q:q + rb, c0:c0 + LANE]
                acc = acc + win * w_ref[j:j + 1, c0:c0 + LANE]
            y_ref[r0:r0 + rb, c0:c0 + LANE] = acc
    y = y_ref[...] + cb_ref[...]
    mu = jnp.mean(y, axis=-1, keepdims=True)
    yc = y - mu
    var = jnp.mean(yc * yc, axis=-1, keepdims=True)
    yn = yc * lax.rsqrt(var + LN_EPS) * g_ref[...] + b_ref[...]
    o_ref[...] = (yn * _sigmoid(yn)).astype(o_ref.dtype)


def _conv_branch(proj, conv_w, conv_b, ln_g, ln_b, seq):
    t = proj.shape[0]
    taps, _, ca = conv_w.shape
    assert taps - 1 <= CONV_HALO
    ts = _tile(seq, 256)
    hb = ts // CONV_HALO
    kern = functools.partial(_conv_kernel, ts=ts, ca=ca, taps=taps, seq=seq)
    return pl.pallas_call(
        kern,
        grid=(t // ts,),
        in_specs=[pl.BlockSpec((ts, 2 * ca), lambda i: (i, 0)),
                  pl.BlockSpec((CONV_HALO, 2 * ca), lambda i: (jnp.maximum(i * hb - 1, 0), 0)),
                  pl.BlockSpec((taps, ca), lambda i: (0, 0)),
                  pl.BlockSpec((1, ca), lambda i: (0, 0)),
                  pl.BlockSpec((1, ca), lambda i: (0, 0)),
                  pl.BlockSpec((1, ca), lambda i: (0, 0))],
        out_specs=pl.BlockSpec((ts, ca), lambda i: (i, 0)),
        out_shape=jax.ShapeDtypeStruct((t, ca), BF),
        scratch_shapes=[pltpu.VMEM((ts + CONV_HALO, ca), F32),
                        pltpu.VMEM((SUBLANE - 1, ts + CONV_HALO - SUBLANE, ca), F32),
                        pltpu.VMEM((ts, ca), F32)],
        compiler_params=_params("parallel"),
        name="conv_branch",
    )(proj, proj, conv_w.reshape(taps, ca), conv_b.reshape(1, ca), ln_g.reshape(1, ca), ln_b.reshape(1, ca))


def _token_shift(cur_ref, halo_ref, mu, first):
    cur = cur_ref[...].astype(F32)
    last = halo_ref[SHIFT_HALO - 1:SHIFT_HALO, :].astype(F32)
    last = jnp.where(first, 0.0, last)
    row = lax.broadcasted_iota(jnp.int32, cur.shape, 0)
    prev = jnp.where(row == 0, last, pltpu.roll(cur, 1, axis=0))
    return cur + (prev - cur) * mu


def _prep_kernel(*refs, ts, seq, cb, r_decay, has_vres):
    (r_ref, k_ref, v_ref, l_ref, rh_ref, kh_ref, vh_ref, lh_ref, mur_ref, muk_ref, muv_ref, mul_ref,
     w2_ref, a2_ref, g2_ref, w0_ref, a0_ref, kk_ref, ka_ref, e_ref, et_ref) = refs[:21]
    rest = refs[21:]
    if has_vres:
        v2_ref, v0_ref, vf_ref = rest[:3]
        rest = rest[3:]
    ro_ref, ldo_ref, ko_ref, vo_ref, kko_ref, ao_ref, go_ref = rest

    first = (pl.program_id(0) * ts) % seq == 0
    r = _token_shift(r_ref, rh_ref, mur_ref[...], first)
    k = _token_shift(k_ref, kh_ref, muk_ref[...], first)
    v = _token_shift(v_ref, vh_ref, muv_ref[...], first)
    lz = _token_shift(l_ref, lh_ref, mul_ref[...], first)

    l1 = lz[:, :LORA_W]
    lane = lax.broadcasted_iota(jnp.int32, l1.shape, 1)
    t1 = jnp.where(lane < r_decay, jnp.tanh(l1), l1).astype(BF)
    wpre = w0_ref[...] + _dot(t1, w2_ref[...])
    ldo_ref[...] = -DECAY_SCALE * _sigmoid(wpre)
    a = _sigmoid(a0_ref[...] + _dot(t1, a2_ref[...]))
    go_ref[...] = _dot(_sigmoid(lz[:, LORA_W:]).astype(BF), g2_ref[...]).astype(go_ref.dtype)
    if has_vres:
        v = v + (vf_ref[...].astype(F32) - v) * _sigmoid(v0_ref[...] + _dot(t1, v2_ref[...]))

    kk = k * kk_ref[...]
    sq_hi, sq_lo = _split2(kk * kk)
    ss = _dot(sq_hi, e_ref[...]) + _dot(sq_lo, e_ref[...])
    inv = lax.rsqrt(jnp.maximum(ss, 1e-24))
    inv_hi, inv_mid, inv_lo = _split3(inv)
    invb = _dot(inv_hi, et_ref[...]) + _dot(inv_mid, et_ref[...]) + _dot(inv_lo, et_ref[...])
    ro_ref[...] = r.astype(ro_ref.dtype)
    ko_ref[...] = (k * (1.0 + (a - 1.0) * ka_ref[...])).astype(ko_ref.dtype)
    vo_ref[...] = v.astype(vo_ref.dtype)
    kko_ref[...] = (kk * invb).astype(kko_ref.dtype)
    ao_ref[...] = a.astype(ao_ref.dtype)


def _rwkv_prep(proj, mu_rkv, mu_lora, w2p, a2p, g2, w0, a0, k_k, k_a, vres, seq, cb, col_r, col_lora, r_decay):
    t = proj.shape[0]
    ts = _tile(seq, 256)
    hb = ts // SHIFT_HALO
    nh = cb // HEAD
    hp = max(nh, LANE)
    head_of = jnp.arange(cb, dtype=jnp.int32) // HEAD
    e = (head_of[:, None] == jnp.arange(hp, dtype=jnp.int32)[None, :]).astype(BF)
    et = e.T
    lw = 2 * LORA_W
    cr = col_r // cb
    cl = col_lora // lw
    halo = lambda i: jnp.maximum(i * hb - 1, 0)
    row = lambda n: pl.BlockSpec((1, n), lambda i: (0, 0))
    full = lambda a, b: pl.BlockSpec((a, b), lambda i: (0, 0))
    in_specs = [pl.BlockSpec((ts, cb), lambda i: (i, cr)),
                pl.BlockSpec((ts, cb), lambda i: (i, cr + 1)),
                pl.BlockSpec((ts, cb), lambda i: (i, cr + 2)),
                pl.BlockSpec((ts, lw), lambda i: (i, cl)),
                pl.BlockSpec((SHIFT_HALO, cb), lambda i: (halo(i), cr)),
                pl.BlockSpec((SHIFT_HALO, cb), lambda i: (halo(i), cr + 1)),
                pl.BlockSpec((SHIFT_HALO, cb), lambda i: (halo(i), cr + 2)),
                pl.BlockSpec((SHIFT_HALO, lw), lambda i: (halo(i), cl)),
                row(cb), row(cb), row(cb), row(lw),
                full(LORA_W, cb), full(LORA_W, cb), full(LORA_W, cb),
                row(cb), row(cb), row(cb), row(cb), full(cb, hp), full(hp, cb)]
    args = [proj, proj, proj, proj, proj, proj, proj, proj,
            mu_rkv[0:cb].reshape(1, cb), mu_rkv[cb:2 * cb].reshape(1, cb), mu_rkv[2 * cb:3 * cb].reshape(1, cb),
            mu_lora.reshape(1, lw), w2p, a2p, g2, w0.reshape(1, cb), a0.reshape(1, cb),
            k_k.reshape(1, cb), k_a.reshape(1, cb), e, et]
    if vres is not None:
        v2p, v0, v_first = vres
        in_specs += [full(LORA_W, cb), row(cb), pl.BlockSpec((ts, cb), lambda i: (i, 0))]
        args += [v2p, v0.reshape(1, cb), v_first]
    tile = pl.BlockSpec((ts, cb), lambda i: (i, 0))
    kern = functools.partial(_prep_kernel, ts=ts, seq=seq, cb=cb, r_decay=r_decay, has_vres=vres is not None)
    out = lambda dt: jax.ShapeDtypeStruct((t, cb), dt)
    return pl.pallas_call(
        kern,
        grid=(t // ts,),
        in_specs=in_specs,
        out_specs=[tile] * 7,
        out_shape=[out(BF), out(F32), out(BF), out(BF), out(BF), out(BF), out(BF)],
        compiler_params=_params("parallel"),
        name="rwkv_prep",
    )(*args)


def _scan_kernel(r_ref, ld_ref, k_ref, v_ref, kk_ref, a_ref, g_ref, rk_ref, gg_ref, gb_ref, o_ref, s_ref,
                 *, nchunk, ng):
    c = CHUNK
    lg = SCAN_LANES
    nq = lg // HEAD
    groups = range(ng)

    @pl.when(pl.program_id(2) == 0)
    def _():
        s_ref[...] = jnp.zeros_like(s_ref)

    row = lax.broadcasted_iota(jnp.int32, (lg, lg), 0)
    col = lax.broadcasted_iota(jnp.int32, (lg, lg), 1)
    same = (row // c) == (col // HEAD)
    bd = same.astype(F32)
    strict = same & (row > col)
    incl = same & (row >= col)
    eye = (row == col).astype(F32)
    avg = (bd * (1.0 / HEAD)).astype(BF)
    ones = bd.astype(BF)
    tr = lax.broadcasted_iota(jnp.int32, (c, c), 0)
    tc = lax.broadcasted_iota(jnp.int32, (c, c), 1)
    ltri = (tr >= tc).astype(BF)

    def stack(x):
        return jnp.concatenate([x] * nq, axis=0) * bd

    def unstack(y):
        out = y[0:c]
        for qi in range(1, nq):
            out = out + y[qi * c:(qi + 1) * c]
        return out

    def body(ci, carry):
        sl = pl.ds(pl.multiple_of(ci * c, c), c)
        ld = ld_ref[sl, :]
        l_hi, l_mid, l_lo = _split3(ld)
        cum = _dot(ltri, l_hi) + _dot(ltri, l_mid) + _dot(ltri, l_lo)
        dinc = jnp.exp(cum)
        dinv = jnp.exp(-cum)
        dexc = jnp.exp(cum - ld)
        kk = kk_ref[sl, :].astype(F32)
        r = r_ref[sl, :].astype(F32)
        k = k_ref[sl, :].astype(F32)
        v = v_ref[sl, :].astype(F32)
        at = -kk * dexc
        rt = r * dinc
        bt = kk * a_ref[sl, :].astype(F32) * dinv
        kt = k * dinv
        dtot = dinc[c - 1:c, :]
        lanes = [slice(gi * lg, (gi + 1) * lg) for gi in groups]

        r_s = [stack(rt[:, ln]) for ln in lanes]
        b_s = [stack(bt[:, ln]) for ln in lanes]
        k_s = [stack(kt[:, ln]) for ln in lanes]
        a_b = [stack(at[:, ln]).astype(BF) for ln in lanes]
        v_b = [stack(v[:, ln]).astype(BF) for ln in lanes]
        r_b = [x.astype(BF) for x in r_s]
        b_b = [x.astype(BF) for x in b_s]
        k_b = [x.astype(BF) for x in k_s]

        a_ab = [jnp.where(strict, _dot_nt(a_b[gi], b_b[gi]), 0.0) for gi in groups]
        a_ak = [jnp.where(strict, _dot_nt(a_b[gi], k_b[gi]), 0.0).astype(BF) for gi in groups]
        a_rb = [jnp.where(incl, _dot_nt(r_b[gi], b_b[gi]), 0.0).astype(BF) for gi in groups]
        a_rk = [jnp.where(incl, _dot_nt(r_b[gi], k_b[gi]), 0.0).astype(BF) for gi in groups]

        tm = [eye + x for x in a_ab]
        pw = a_ab
        for _ in range(c.bit_length() - 2):
            pb = [x.astype(BF) for x in pw]
            pw = [_dot(x, x) for x in pb]
            tm = [tm[gi] + _dot(tm[gi].astype(BF), pw[gi].astype(BF)) for gi in groups]
        tmb = [x.astype(BF) for x in tm]

        akv = [_dot(a_ak[gi], v_b[gi]).astype(BF) for gi in groups]
        wb = [_dot(tmb[gi], a_b[gi]).astype(BF) for gi in groups]
        u0b = [_dot(tmb[gi], akv[gi]).astype(BF) for gi in groups]
        q = [unstack(r_s[gi] + _dot(a_rb[gi], wb[gi])).astype(BF) for gi in groups]
        y0 = [unstack(_dot(a_rb[gi], u0b[gi]) + _dot(a_rk[gi], v_b[gi])) for gi in groups]
        bd_t = [(b_s[gi] * dtot[:, lanes[gi]]).T.astype(BF) for gi in groups]
        kd_t = [(k_s[gi] * dtot[:, lanes[gi]]).T.astype(BF) for gi in groups]
        m = [(eye * dtot[:, lanes[gi]] + _dot(bd_t[gi], wb[gi])).astype(BF) for gi in groups]
        pm = [_dot(bd_t[gi], u0b[gi]) + _dot(kd_t[gi], v_b[gi]) for gi in groups]

        s0 = [s_ref[gi].astype(BF) for gi in groups]
        yt = [_dot(q[gi], s0[gi]) + y0[gi] for gi in groups]
        for gi in groups:
            s_ref[gi] = _dot(m[gi], s0[gi]) + pm[gi]

        for gi in groups:
            ln = lanes[gi]
            mu = _dot(yt[gi].astype(BF), avg)
            yc = yt[gi] - mu
            var = _dot((yc * yc).astype(BF), avg)
            yn = yc * lax.rsqrt(var + GN_EPS) * gg_ref[:, ln] + gb_ref[:, ln]
            bonus = _dot((r[:, ln] * k[:, ln] * rk_ref[:, ln]).astype(BF), ones)
            o_ref[sl, ln] = ((yn + bonus * v[:, ln]) * g_ref[sl, ln].astype(F32)).astype(o_ref.dtype)
        return carry

    lax.fori_loop(0, nchunk, body, 0)


def _wkv_scan(r, ld, k, v, kk, a, g, r_k, gn_g, gn_b, bsz, seq):
    t, cb = r.shape
    assert cb % SCAN_LANES == 0
    ng = min(cb // SCAN_LANES, SCAN_GROUPS)
    lb = ng * SCAN_LANES
    sb = _tile(seq, 256)
    nsb = seq // sb
    tile = pl.BlockSpec((sb, lb), lambda b, h, s: (b * nsb + s, h))
    vec = pl.BlockSpec((1, lb), lambda b, h, s: (0, h))
    kern = functools.partial(_scan_kernel, nchunk=sb // CHUNK, ng=ng)
    return pl.pallas_call(
        kern,
        grid=(bsz, cb // lb, nsb),
        in_specs=[tile] * 7 + [vec] * 3,
        out_specs=tile,
        out_shape=jax.ShapeDtypeStruct((t, cb), BF),
        scratch_shapes=[pltpu.VMEM((ng, SCAN_LANES, SCAN_LANES), F32)],
        compiler_params=_params("parallel", "parallel", "arbitrary"),
        name="wkv_scan",
    )(r, ld, k, v, kk, a, g, r_k.reshape(1, cb), gn_g.reshape(1, cb), gn_b.reshape(1, cb))


def _mixa_kernel(a_ref, y_ref, ga_ref, gb_ref, wc_ref, wr_ref, o_ref):
    ya = _dot(a_ref[...], wc_ref[...])
    yb = _dot(y_ref[...], wr_ref[...])
    m = _sigmoid(ga_ref[...].astype(F32)) * ya + _sigmoid(gb_ref[...].astype(F32)) * yb
    o_ref[...] = m.astype(o_ref.dtype)


def _mix_branches(a_conv, yg, proj, wc, wr, col_gate):
    t, ca = a_conv.shape
    cb = yg.shape[1]
    d = wc.shape[1]
    tm = _tile(t, 512)
    cg = col_gate // d
    return pl.pallas_call(
        _mixa_kernel,
        grid=(t // tm,),
        in_specs=[pl.BlockSpec((tm, ca), lambda i: (i, 0)),
                  pl.BlockSpec((tm, cb), lambda i: (i, 0)),
                  pl.BlockSpec((tm, d), lambda i: (i, cg)),
                  pl.BlockSpec((tm, d), lambda i: (i, cg + 1)),
                  pl.BlockSpec((ca, d), lambda i: (0, 0)),
                  pl.BlockSpec((cb, d), lambda i: (0, 0))],
        out_specs=pl.BlockSpec((tm, d), lambda i: (i, 0)),
        out_shape=jax.ShapeDtypeStruct((t, d), BF),
        compiler_params=_params("parallel"),
        name="mix_branches",
    )(a_conv, yg, proj, proj, wc, wr)


def _mixb_kernel(m_ref, x_ref, gt_ref, sh_ref, sc_ref, g_ref, wo_ref, rwh_ref, rwl_ref, rb_ref,
                 xo_ref, ho_ref, ti_ref, tw_ref, *, ne):
    x = x_ref[...] + gt_ref[...] * _dot(m_ref[...], wo_ref[...])
    xo_ref[...] = x
    ms = jnp.mean(x * x, axis=-1, keepdims=True)
    h = x * lax.rsqrt(ms + RMS_EPS) * g_ref[...] * (1.0 + sc_ref[...]) + sh_ref[...]
    ho_ref[...] = h
    h_hi, h_lo = _split2(h)
    logits = _dot(h_hi, rwh_ref[...]) + _dot(h_lo, rwh_ref[...]) + _dot(h_hi, rwl_ref[...]) + rb_ref[...]
    lane = lax.broadcasted_iota(jnp.int32, logits.shape, 1)
    neg = jnp.float32(-jnp.inf)
    cur = jnp.where(lane < ne, logits, neg)
    vals, idxs = [], []
    for _ in range(TOP_K):
        mx = jnp.max(cur, axis=-1, keepdims=True)
        ix = jnp.min(jnp.where(cur == mx, lane, LANE), axis=-1, keepdims=True)
        vals.append(mx)
        idxs.append(ix)
        cur = jnp.where(lane == ix, neg, cur)
    es = [jnp.exp(vk - vals[0]) for vk in vals]
    den = es[0]
    for ek in es[1:]:
        den = den + ek
    ti = jnp.zeros(logits.shape, jnp.int32)
    tw = jnp.zeros(logits.shape, F32)
    for kq in range(TOP_K):
        ti = jnp.where(lane == kq, idxs[kq], ti)
        tw = jnp.where(lane == kq, es[kq] / den, tw)
    ti_ref[...] = ti
    tw_ref[...] = tw


def _out_and_route(m, x2, mod3, g2, wo, rw_hi, rw_lo, rb, seq, ne):
    t, d = x2.shape
    tm = _tile(seq, 512)
    bidx = lambda i: (i * tm) // seq
    modv = lambda which: pl.BlockSpec((None, 1, d), lambda i: (bidx(i), 0, which))
    kern = functools.partial(_mixb_kernel, ne=ne)
    return pl.pallas_call(
        kern,
        grid=(t // tm,),
        in_specs=[pl.BlockSpec((tm, d), lambda i: (i, 0)),
                  pl.BlockSpec((tm, d), lambda i: (i, 0)),
                  modv(2), modv(3), modv(4),
                  pl.BlockSpec((1, d), lambda i: (0, 0)),
                  pl.BlockSpec((d, d), lambda i: (0, 0)),
                  pl.BlockSpec((d, LANE), lambda i: (0, 0)),
                  pl.BlockSpec((d, LANE), lambda i: (0, 0)),
                  pl.BlockSpec((1, LANE), lambda i: (0, 0))],
        out_specs=[pl.BlockSpec((tm, d), lambda i: (i, 0)),
                   pl.BlockSpec((tm, d), lambda i: (i, 0)),
                   pl.BlockSpec((tm, LANE), lambda i: (i, 0)),
                   pl.BlockSpec((tm, LANE), lambda i: (i, 0))],
        out_shape=[jax.ShapeDtypeStruct((t, d), F32), jax.ShapeDtypeStruct((t, d), F32),
                   jax.ShapeDtypeStruct((t, LANE), jnp.int32), jax.ShapeDtypeStruct((t, LANE), F32)],
        compiler_params=_params("parallel"),
        name="out_and_route",
    )(m, x2, mod3, mod3, mod3, g2.reshape(1, d), wo, rw_hi, rw_lo, rb)


def _expert_kernel(be_ref, src_cur, src_nxt, h_hbm, wgu_ref, bgu_ref, wd_ref, bd_ref, o_ref,
                   x0, x1, gsem, *, bm, de, nb):
    i = pl.program_id(0)

    def start_gather(src_ref, xb, sem):
        for rw in range(bm):
            pltpu.make_async_copy(h_hbm.at[pl.ds(src_ref[0, rw], 1)], xb.at[pl.ds(rw, 1)], sem).start()

    def wait_gather(xb, sem):
        pltpu.make_async_copy(h_hbm.at[pl.ds(0, bm)], xb, sem).wait()

    def step(xc, xo, gc, go):
        @pl.when(i == 0)
        def _():
            start_gather(src_cur, xc, gc)

        wait_gather(xc, gc)

        @pl.when(be_ref[i] >= 0)
        def _():
            start_gather(src_nxt, xo, go)

        x = xc[...].astype(BF)
        gu = _dot(x, wgu_ref[...]) + bgu_ref[...]
        gate = jnp.minimum(gu[:, :de], SWIGLU_LIMIT)
        up = jnp.clip(gu[:, de:], -SWIGLU_LIMIT, SWIGLU_LIMIT)
        act = ((up + 1.0) * gate * _sigmoid(SWIGLU_ALPHA * gate)).astype(BF)
        o_ref[...] = _dot(act, wd_ref[...]) + bd_ref[...]

        @pl.when(i == nb - 1)
        def _():
            wait_gather(xo, go)

    @pl.when(i % 2 == 0)
    def _():
        step(x0, x1, gsem.at[0], gsem.at[1])

    @pl.when(i % 2 == 1)
    def _():
        step(x1, x0, gsem.at[1], gsem.at[0])


def _experts(h, block_expert, src_ext, wgu, bgu, wd, bd, layer, bm):
    t, d = h.shape
    nl, ne, _, de2 = wgu.shape
    de = de2 // 2
    nb = block_expert.shape[0]
    kern = functools.partial(_expert_kernel, bm=bm, de=de, nb=nb)
    smem_row = lambda off: pl.BlockSpec((None, 1, bm), lambda i, be: (i + off, 0, 0), memory_space=pltpu.SMEM)
    grid_spec = pltpu.PrefetchScalarGridSpec(
        num_scalar_prefetch=1,
        grid=(nb,),
        in_specs=[smem_row(0), smem_row(1),
                  pl.BlockSpec(memory_space=pl.ANY),
                  pl.BlockSpec((None, None, d, de2), lambda i, be: (layer, be[i], 0, 0)),
                  pl.BlockSpec((None, None, 1, de2), lambda i, be: (layer, be[i], 0, 0)),
                  pl.BlockSpec((None, None, de, d), lambda i, be: (layer, be[i], 0, 0)),
                  pl.BlockSpec((None, None, 1, d), lambda i, be: (layer, be[i], 0, 0))],
        out_specs=pl.BlockSpec((bm, d), lambda i, be: (i, 0)),
        scratch_shapes=[pltpu.VMEM((bm, d), F32), pltpu.VMEM((bm, d), F32), pltpu.SemaphoreType.DMA((2,))],
    )
    src3 = src_ext.reshape(nb + 1, 1, bm)
    return pl.pallas_call(
        kern,
        grid_spec=grid_spec,
        out_shape=jax.ShapeDtypeStruct((nb * bm, d), F32),
        compiler_params=_params("arbitrary"),
        name="experts",
    )(block_expert, src3, src3, h, wgu, bgu.reshape(nl, ne, 1, de2), wd, bd.reshape(nl, ne, 1, d))


def _routing_tables(top_idx, ne, bm, tm):
    t = top_idx.shape[0]
    m = t * TOP_K
    flat_e = top_idx.reshape(m)
    assert ne * m < 2 ** 31
    order = jnp.sort(flat_e * m + jnp.arange(m, dtype=jnp.int32)) % m
    counts = jnp.sum((flat_e[:, None] == jnp.arange(ne, dtype=jnp.int32)[None, :]).astype(jnp.int32), axis=0)
    padded = (counts + bm - 1) // bm * bm
    starts = jnp.cumsum(counts) - counts
    pad_ends = jnp.cumsum(padded)
    pad_starts = pad_ends - padded
    nb = -(-m // bm) + ne
    block_start = jnp.arange(nb, dtype=jnp.int32) * bm
    block_expert = jnp.sum((pad_ends[None, :] <= block_start[:, None]).astype(jnp.int32), axis=1)
    block_expert = jnp.minimum(block_expert, ne - 1)
    rank = (block_start - pad_starts[block_expert])[:, None] + jnp.arange(bm, dtype=jnp.int32)[None, :]
    valid = (rank >= 0) & (rank < counts[block_expert][:, None])
    assign = order[jnp.clip(starts[block_expert][:, None] + rank, 0, m - 1)]
    src_tok = jnp.where(valid, assign // TOP_K, 0).astype(jnp.int32)
    src_ext = jnp.concatenate([src_tok, jnp.zeros((1, bm), jnp.int32)], axis=0)
    inv = jnp.argsort(order).astype(jnp.int32)
    pos = (pad_starts - starts)[flat_e] + inv
    pos = pos.reshape(t // tm, tm, TOP_K).transpose(0, 2, 1).reshape(t // tm, TOP_K * tm)
    pos_ext = jnp.concatenate([pos, jnp.zeros((1, TOP_K * tm), jnp.int32)], axis=0).astype(jnp.int32)
    return block_expert, src_ext, pos_ext


def _combine_kernel(pos_cur, pos_nxt, x_ref, tw_ref, gt_ref, fg_ref, y_hbm, o_ref, y0, y1, gsem,
                    *, tm, nt, final):
    i = pl.program_id(0)
    nrow = TOP_K * tm

    def start_gather(pos_ref, yb, sem):
        for rw in range(nrow):
            pltpu.make_async_copy(y_hbm.at[pl.ds(pos_ref[0, rw], 1)], yb.at[pl.ds(rw, 1)], sem).start()

    def wait_gather(yb, sem):
        pltpu.make_async_copy(y_hbm.at[pl.ds(0, nrow)], yb, sem).wait()

    def step(yc, yo, gc, go):
        @pl.when(i == 0)
        def _():
            start_gather(pos_cur, yc, gc)

        wait_gather(yc, gc)

        @pl.when(pos_nxt[0, 0] >= 0)
        def _():
            start_gather(pos_nxt, yo, go)

        tw = tw_ref[...]
        acc = tw[:, 0:1] * yc[0:tm, :]
        for kq in range(1, TOP_K):
            acc = acc + tw[:, kq:kq + 1] * yc[kq * tm:(kq + 1) * tm, :]
        x = x_ref[...] + gt_ref[...] * acc
        if final:
            ms = jnp.mean(x * x, axis=-1, keepdims=True)
            x = x * lax.rsqrt(ms + RMS_EPS) * fg_ref[...]
        o_ref[...] = x

        @pl.when(i == nt - 1)
        def _():
            wait_gather(yo, go)

    @pl.when(i % 2 == 0)
    def _():
        step(y0, y1, gsem.at[0], gsem.at[1])

    @pl.when(i % 2 == 1)
    def _():
        step(y1, y0, gsem.at[1], gsem.at[0])


def _combine(x2, y_sorted, pos_ext, top_w, mod3, final_g, seq, tm, final):
    t, d = x2.shape
    nt = t // tm
    bidx = lambda i: (i * tm) // seq
    kern = functools.partial(_combine_kernel, tm=tm, nt=nt, final=final)
    smem_row = lambda off: pl.BlockSpec((None, 1, TOP_K * tm), lambda i: (i + off, 0, 0), memory_space=pltpu.SMEM)
    pos3 = pos_ext.reshape(nt + 1, 1, TOP_K * tm)
    return pl.pallas_call(
        kern,
        grid=(nt,),
        in_specs=[smem_row(0), smem_row(1),
                  pl.BlockSpec((tm, d), lambda i: (i, 0)),
                  pl.BlockSpec((tm, LANE), lambda i: (i, 0)),
                  pl.BlockSpec((None, 1, d), lambda i: (bidx(i), 0, 5)),
                  pl.BlockSpec((1, d), lambda i: (0, 0)),
                  pl.BlockSpec(memory_space=pl.ANY)],
        out_specs=pl.BlockSpec((tm, d), lambda i: (i, 0)),
        out_shape=jax.ShapeDtypeStruct((t, d), F32),
        scratch_shapes=[pltpu.VMEM((TOP_K * tm, d), F32), pltpu.VMEM((TOP_K * tm, d), F32),
                        pltpu.SemaphoreType.DMA((2,))],
        compiler_params=_params("arbitrary"),
        name="combine",
    )(pos3, pos3, x2, top_w, mod3, final_g.reshape(1, d), y_sorted)


def kernel(x, c, norm1_g, norm2_g, w_mod, b_mod, w_in, conv_w, conv_b, conv_ln_g, conv_ln_b, w_conv_proj, mu_shift, w0, w2, a0, a2, g2, k_k, k_a, r_k, gn_g, gn_b, w_rwkv_proj, v0, v1, mu_v, v2, w_out, router_w, router_b, w_gate_up, b_gate_up, w_down, b_down, final_g):
    bsz, seq, d = x.shape
    depth = w_mod.shape[0]
    ca = conv_w.shape[-1]
    cb = w0.shape[1]
    r_decay, r_aaa, r_gate, r_mv = w2.shape[1], a2.shape[1], g2.shape[1], v1.shape[2]
    ne = router_w.shape[2]
    assert r_decay + r_aaa + r_mv == LORA_W and r_gate == LORA_W
    assert (2 * ca) % cb == 0 and cb == d and ne <= LANE
    t = bsz * seq
    c0 = 2 * ca
    c_lora = c0 + 3 * cb
    c_gate = c_lora + r_decay + r_aaa + r_gate
    col_gate = c0 + 3 * cb
    col_lora = col_gate + 2 * d
    bm = 256
    tmc = _tile(seq, 256)

    mod = _modulation(c, w_mod, b_mod)
    wgu_all = w_gate_up.astype(BF)
    wd_all = w_down.astype(BF)
    x2 = x.reshape(t, d)
    v_first = None
    for l in range(depth):
        mod3 = mod[l].reshape(bsz, 1, 6 * d)
        wl = w_in[l]
        if l == 0:
            v1_cols = jnp.zeros((d, r_mv), F32)
            mu_mv = jnp.zeros((r_mv,), F32)
        else:
            v1_cols = v1[l - 1]
            mu_mv = mu_v[l - 1]
        w_all = jnp.concatenate(
            [wl[:, :c_lora], wl[:, c_gate:], wl[:, c_lora:c_lora + r_decay + r_aaa], v1_cols,
             wl[:, c_lora + r_decay + r_aaa:c_gate]], axis=1).astype(BF)
        mu_l = mu_shift[l]
        mu_lora = jnp.concatenate([mu_l[3 * cb:3 * cb + r_decay + r_aaa], mu_mv, mu_l[3 * cb + r_decay + r_aaa:]])
        zpad = lambda top, w, bot: jnp.concatenate(
            [jnp.zeros((top, cb), F32), w, jnp.zeros((bot, cb), F32)], axis=0).astype(BF)
        w2p = zpad(0, w2[l], LORA_W - r_decay)
        a2p = zpad(r_decay, a2[l], r_mv)

        proj = _inproj(x2, mod3, norm1_g[l], w_all, seq)
        a_conv = _conv_branch(proj, conv_w[l], conv_b[l], conv_ln_g[l], conv_ln_b[l], seq)
        vres = None if l == 0 else (zpad(r_decay + r_aaa, v2[l - 1], 0), v0[l - 1], v_first)
        r, ld, k, v, kk, a, g = _rwkv_prep(proj, mu_l[:3 * cb], mu_lora, w2p, a2p, g2[l].astype(BF), w0[l], a0[l],
                                           k_k[l], k_a[l], vres, seq, cb, c0, col_lora, r_decay)
        if l == 0:
            v_first = v
        yg = _wkv_scan(r, ld, k, v, kk, a, g, r_k[l], gn_g[l], gn_b[l], bsz, seq)
        mixed = _mix_branches(a_conv, yg, proj, w_conv_proj[l].astype(BF), w_rwkv_proj[l].astype(BF), col_gate)

        rw = jnp.pad(router_w[l], ((0, 0), (0, LANE - ne)))
        rw_hi = rw.astype(BF)
        rw_lo = (rw - rw_hi.astype(F32)).astype(BF)
        rb = jnp.pad(router_b[l], (0, LANE - ne)).reshape(1, LANE)
        x2, h, top_i, top_w = _out_and_route(mixed, x2, mod3, norm2_g[l], w_out[l].astype(BF), rw_hi, rw_lo, rb,
                                             seq, ne)
        block_expert, src_ext, pos_ext = _routing_tables(top_i[:, :TOP_K], ne, bm, tmc)
        y_sorted = _experts(h, block_expert, src_ext, wgu_all, b_gate_up, wd_all, b_down, l, bm)
        x2 = _combine(x2, y_sorted, pos_ext, top_w, mod3, final_g, seq, tmc, final=(l == depth - 1))
    return x2.reshape(bsz, seq, d)
```
